```python
import math
import jax
import jax.numpy as jnp
from jax import lax
import numpy as np

D_MODEL = 1024
BATCH = 1
SEQ = 16384
DEPTH = 4

GRID_W = 64
CTX_LEN = 256
Q_BLOCK = 128
ROPE_BASE = 10000.0
NORM_EPS = 1e-6

W_DIFF = 256
W_SSD = 256
W_S5 = 256
W_GQA = 256
D_MIX = W_DIFF + W_SSD + W_S5 + W_GQA

DIFF_HEADS = 4
DIFF_QK_DIM = 32
DIFF_V_DIM = 2 * DIFF_QK_DIM

SSD_HEAD_DIM = 64
SSD_HEADS = W_SSD // SSD_HEAD_DIM
SSD_GROUPS = 2
SSD_STATE = 128
SSD_CONV = 3
SSD_CHUNK = 128
SSD_XBC = W_SSD + 2 * SSD_GROUPS * SSD_STATE

S5_GROUP = 16
S5_GROUPS = W_S5 // S5_GROUP
S5_STATE = 64

GQA_HEADS = 4
GQA_KV_HEADS = 2
GQA_REP = GQA_HEADS // GQA_KV_HEADS
GQA_HEAD_DIM = W_GQA // GQA_HEADS

N_DIFF_IN = 3 * W_DIFF
N_SSD_IN = W_SSD + SSD_XBC + 2 * SSD_HEADS
N_S5_IN = W_S5
N_GQA_IN = (GQA_HEADS + 2 * GQA_KV_HEADS) * GQA_HEAD_DIM
D_IN = N_DIFF_IN + N_SSD_IN + N_S5_IN + N_GQA_IN

N_EXPERTS = 32
TOP_K = 4
D_EXPERT = D_MODEL
SWIGLU_LIMIT = 7.0
SWIGLU_ALPHA = 1.702
MOE_BLOCK = 128

kernel_name = 'hybrid_parallel_heads_diffusion_moe'


def rms_norm(x, w):
    xf = x.astype(jnp.float32)
    y = xf * lax.rsqrt(jnp.mean(xf * xf, axis=-1, keepdims=True) + NORM_EPS)
    return (y * w.astype(jnp.float32)).astype(x.dtype)


def modulate(h, shift, scale):
    return h * (1.0 + scale) + shift


def axial_rope_tables(row, col, head_dim):
    axis_dim = head_dim // 2
    inv_freq = ROPE_BASE ** (-jnp.arange(0, axis_dim, 2, dtype=jnp.float32) / axis_dim)
    ang_r = row.astype(jnp.float32)[:, None] * inv_freq
    ang_c = col.astype(jnp.float32)[:, None] * inv_freq
    return (jnp.cos(ang_r), jnp.sin(ang_r), jnp.cos(ang_c), jnp.sin(ang_c))


def _rotate_half(x, cos, sin):
    x1, x2 = jnp.split(x.astype(jnp.float32), 2, axis=-1)
    return jnp.concatenate([x1 * cos - x2 * sin, x1 * sin + x2 * cos], axis=-1)


def apply_axial_rope(x, rope):
    cos_r, sin_r, cos_c, sin_c = rope

    def expand(t):
        return t.reshape((1, t.shape[0]) + (1,) * (x.ndim - 3) + (t.shape[1],))

    x_row, x_col = jnp.split(x, 2, axis=-1)
    out = jnp.concatenate([_rotate_half(x_row, expand(cos_r), expand(sin_r)),
                           _rotate_half(x_col, expand(cos_c), expand(sin_c))], axis=-1)
    return out.astype(x.dtype)


def sweep_query_blocks(fn, q):
    b, t = q.shape[:2]
    nb = t // Q_BLOCK
    qb = jnp.moveaxis(q.reshape((b, nb, Q_BLOCK) + q.shape[2:]), 1, 0)
    ob = lax.map(fn, qb)
    return jnp.moveaxis(ob, 0, 1).reshape((b, t) + ob.shape[3:])


def diff_attention_mixer(uc, ul, rope, lq1, lk1, lq2, lk2, subln_w, lam_init, with_ctx):
    f32 = jnp.float32
    lam = (jnp.exp(jnp.sum(lq1.astype(f32) * lk1.astype(f32)))
           - jnp.exp(jnp.sum(lq2.astype(f32) * lk2.astype(f32))) + lam_init)
    scale = DIFF_QK_DIM ** -0.5

    def split_heads(u):
        b, t = u.shape[:2]
        q, k, v = jnp.split(u, 3, axis=-1)
        shp = (b, t, DIFF_HEADS, 2, DIFF_QK_DIM)
        return q.reshape(shp), k.reshape(shp), v.reshape(b, t, DIFF_HEADS, DIFF_V_DIM)

    qc, kc, vc = split_heads(uc)
    ql, kl, vl = split_heads(ul)
    ql = apply_axial_rope(ql, rope)
    kl = apply_axial_rope(kl, rope)
    k_all = jnp.concatenate([kc, kl], axis=1)
    v_all = jnp.concatenate([vc, vl], axis=1)

    def attend(q, k, v):
        s = jnp.einsum('bqhmd,bkhmd->bhmqk', q, k).astype(f32) * scale
        p = jax.nn.softmax(s, axis=-1)
        a = p[:, :, 0] - lam * p[:, :, 1]
        o = jnp.einsum('bhqk,bkhe->bqhe', a.astype(v.dtype), v)
        o = rms_norm(o, subln_w) * (1.0 - lam_init)
        return o.reshape(o.shape[:2] + (W_DIFF,))

    lat = sweep_query_blocks(lambda q: attend(q, k_all, v_all), ql)
    ctx_out = attend(qc, kc, vc) if with_ctx else None
    return ctx_out, lat


def depthwise_conv(x, w, bias):
    width = w.shape[0]
    y = lax.conv_general_dilated(x, w[:, None, :].astype(x.dtype), window_strides=(1,),
                                 padding=[(width // 2, width // 2)],
                                 dimension_numbers=('NWC', 'WIO', 'NWC'),
                                 feature_group_count=x.shape[-1])
    return y + bias.astype(x.dtype)


def segsum(x):
    t = x.shape[-1]
    cs = jnp.cumsum(x, axis=-1)
    diff = cs[..., :, None] - cs[..., None, :]
    return jnp.where(jnp.tril(jnp.ones((t, t), dtype=bool)), diff, -jnp.inf)


def ssd_scan(x, dt, a, bm, cm, init_state):
    b, t, h, p = x.shape
    nc = t // SSD_CHUNK
    rep = h // bm.shape[2]
    bm = jnp.repeat(bm, rep, axis=2)
    cm = jnp.repeat(cm, rep, axis=2)

    def chunk(v):
        return v.reshape((b, nc, SSD_CHUNK) + v.shape[2:])

    xdt = chunk(x * dt[..., None])
    bm, cm = chunk(bm), chunk(cm)
    adt = jnp.moveaxis(chunk(dt * a), -1, 1)
    a_cum = jnp.cumsum(adt, axis=-1)
    l_mat = jnp.exp(segsum(adt))
    y_diag = jnp.einsum('bclhn,bcshn,bhcls,bcshp->bclhp', cm, bm, l_mat, xdt)
    decay_states = jnp.exp(a_cum[..., -1:] - a_cum)
    states = jnp.einsum('bclhn,bhcl,bclhp->bchpn', bm, decay_states, xdt)
    states = jnp.concatenate([init_state[:, None], states], axis=1)
    decay_chunk = jnp.exp(segsum(jnp.pad(a_cum[..., -1], ((0, 0), (0, 0), (1, 0)))))
    new_states = jnp.einsum('bhzc,bchpn->bzhpn', decay_chunk, states)
    y_off = jnp.einsum('bclhn,bchpn,bhcl->bclhp', cm, new_states[:, :-1], jnp.exp(a_cum))
    return (y_diag + y_off).reshape(b, t, h, p), new_states[:, -1]


def ssd_mixer(uc, ul, conv_w, conv_b, dt_bias, a_log, d_skip, norm_w, with_ctx):
    f32 = jnp.float32
    a = -jnp.exp(a_log.astype(f32))
    gn = SSD_GROUPS * SSD_STATE

    def prep(u):
        b, t = u.shape[:2]
        z = u[..., :W_SSD]
        xbc = jax.nn.silu(depthwise_conv(u[..., W_SSD:W_SSD + SSD_XBC], conv_w, conv_b)).astype(f32)
        xs = xbc[..., :W_SSD].reshape(b, t, SSD_HEADS, SSD_HEAD_DIM)
        bm = xbc[..., W_SSD:W_SSD + gn].reshape(b, t, SSD_GROUPS, SSD_STATE)
        cm = xbc[..., W_SSD + gn:].reshape(b, t, SSD_GROUPS, SSD_STATE)
        dt = jax.nn.softplus(u[..., W_SSD + SSD_XBC:].astype(f32).reshape(b, t, 2, SSD_HEADS)
                             + dt_bias.astype(f32))
        return z, xs, bm, cm, dt

    def flip(v):
        return jnp.flip(v, axis=1)

    def bidir(xs, bm, cm, dt, s_f, s_b):
        y_f, fin_f = ssd_scan(xs, dt[:, :, 0], a[0], bm, cm, s_f)
        y_b, fin_b = ssd_scan(flip(xs), flip(dt[:, :, 1]), a[1], flip(bm), flip(cm), s_b)
        return y_f + flip(y_b), fin_f, fin_b

    def finish(y, xs, z):
        y = y + d_skip.astype(f32)[:, None] * xs
        y = y.reshape(y.shape[:2] + (W_SSD,))
        return rms_norm(y * jax.nn.silu(z.astype(f32)), norm_w).astype(z.dtype)

    zc, xc, bc, cc, dtc = prep(uc)
    zl, xl, bl, cl, dtl = prep(ul)
    zero = jnp.zeros((uc.shape[0], SSD_HEADS, SSD_HEAD_DIM, SSD_STATE), f32)
    yc, sc_f, sc_b = bidir(xc, bc, cc, dtc, zero, zero)
    yl, _, _ = bidir(xl, bl, cl, dtl, sc_f, sc_b)
    lat = finish(yl, xl, zl)
    ctx_out = finish(yc, xc, zc) if with_ctx else None
    return ctx_out, lat


def _linear_recurrence(bu, a, h0, reverse):
    edge = -1 if reverse else 0
    bu = bu.at[:, edge].add(a * h0)
    a_seq = jnp.broadcast_to(a, bu.shape)

    def combine(e1, e2):
        a1, b1 = e1
        a2, b2 = e2
        return a1 * a2, a2 * b1 + b2

    _, h = lax.associative_scan(combine, (a_seq, bu), reverse=reverse, axis=1)
    return h


def s5_mixer(uc, ul, lam_re, lam_im, log_dt, b_re, b_im, c_re, c_im, d_skip, w_glu, b_glu, with_ctx):
    f32 = jnp.float32
    lam = lax.complex(lam_re.astype(f32), lam_im.astype(f32))
    step = jnp.exp(log_dt.astype(f32))[..., None]
    a_bar = jnp.exp(lam * step)
    b_mat = lax.complex(b_re.astype(f32), b_im.astype(f32))
    b_bar = ((a_bar - 1.0) / lam)[..., None] * b_mat[None]
    c_mat = lax.complex(c_re.astype(f32), c_im.astype(f32))
    d_grp = d_skip.astype(f32).reshape(S5_GROUPS, S5_GROUP)

    def states(u, h0_f, h0_b):
        b, t = u.shape[:2]
        ug = u.astype(f32).reshape(b, t, S5_GROUPS, S5_GROUP)
        ucx = ug.astype(jnp.complex64)
        bu_f = jnp.einsum('gpc,btgc->btgp', b_bar[0], ucx)
        bu_b = jnp.einsum('gpc,btgc->btgp', b_bar[1], ucx)
        h_f = _linear_recurrence(bu_f, a_bar[0], h0_f, reverse=False)
        h_b = _linear_recurrence(bu_b, a_bar[1], h0_b, reverse=True)
        return ug, h_f, h_b

    def readout(ug, h_f, h_b, dtype):
        y = (jnp.real(jnp.einsum('gcp,btgp->btgc', c_mat[0], h_f))
             + jnp.real(jnp.einsum('gcp,btgp->btgc', c_mat[1], h_b)) + d_grp * ug)
        y = jax.nn.gelu(y.reshape(y.shape[:2] + (W_S5,)))
        return (y * jax.nn.sigmoid(y @ w_glu.astype(f32) + b_glu.astype(f32))).astype(dtype)

    zero = jnp.zeros((uc.shape[0], S5_GROUPS, S5_STATE), jnp.complex64)
    ugc, hc_f, hc_b = states(uc, zero, zero)
    ugl, hl_f, hl_b = states(ul, hc_f[:, -1], hc_b[:, 0])
    lat = readout(ugl, hl_f, hl_b, ul.dtype)
    ctx_out = readout(ugc, hc_f, hc_b, uc.dtype) if with_ctx else None
    return ctx_out, lat


def gqa_mixer(uc, ul, rope, q_norm_w, k_norm_w, with_ctx):
    nq = GQA_HEADS * GQA_HEAD_DIM
    nkv = GQA_KV_HEADS * GQA_HEAD_DIM
    scale = GQA_HEAD_DIM ** -0.5

    def split_heads(u):
        b, t = u.shape[:2]
        q = u[..., :nq].reshape(b, t, GQA_KV_HEADS, GQA_REP, GQA_HEAD_DIM)
        k = u[..., nq:nq + nkv].reshape(b, t, GQA_KV_HEADS, GQA_HEAD_DIM)
        v = u[..., nq + nkv:].reshape(b, t, GQA_KV_HEADS, GQA_HEAD_DIM)
        return rms_norm(q, q_norm_w), rms_norm(k, k_norm_w), v

    qc, kc, vc = split_heads(uc)
    ql, kl, vl = split_heads(ul)
    ql = apply_axial_rope(ql, rope)
    kl = apply_axial_rope(kl, rope)
    k_all = jnp.concatenate([kc, kl], axis=1)
    v_all = jnp.concatenate([vc, vl], axis=1)

    def attend(q, k, v):
        s = jnp.einsum('bqgrd,bkgd->bgrqk', q, k).astype(jnp.float32) * scale
        p = jax.nn.softmax(s, axis=-1)
        o = jnp.einsum('bgrqk,bkgd->bqgrd', p.astype(v.dtype), v)
        return o.reshape(o.shape[:2] + (W_GQA,))

    lat = sweep_query_blocks(lambda q: attend(q, k_all, v_all), ql)
    ctx_out = attend(qc, kc, vc) if with_ctx else None
    return ctx_out, lat


def moe_ffn(h, w_router, b_router, w_gate_up, b_gate_up, w_down, b_down):
    n, d = h.shape
    logits = (h @ w_router).astype(jnp.float32) + b_router.astype(jnp.float32)
    top_val, top_idx = lax.top_k(logits, TOP_K)
    gates = jax.nn.softmax(top_val, axis=-1)
    m = n * TOP_K
    expert = top_idx.reshape(-1)
    token = jnp.repeat(jnp.arange(n, dtype=jnp.int32), TOP_K)
    gate = gates.reshape(-1)
    order = jnp.argsort(expert)
    e_sorted = expert[order]
    counts = jnp.bincount(expert, length=N_EXPERTS)
    padded = (counts + MOE_BLOCK - 1) // MOE_BLOCK * MOE_BLOCK
    start = jnp.cumsum(counts) - counts
    pend = jnp.cumsum(padded)
    pstart = pend - padded
    dest = pstart[e_sorted] + jnp.arange(m, dtype=jnp.int32) - start[e_sorted]
    n_blocks = -(-(m + N_EXPERTS * (MOE_BLOCK - 1)) // MOE_BLOCK)
    cap = n_blocks * MOE_BLOCK
    slot_token = jnp.zeros((cap,), jnp.int32).at[dest].set(token[order])
    slot_gate = jnp.zeros((cap,), jnp.float32).at[dest].set(gate[order])
    block_expert = jnp.minimum(
        jnp.searchsorted(pend, jnp.arange(n_blocks, dtype=jnp.int32) * MOE_BLOCK, side='right'),
        N_EXPERTS - 1)
    xb = h[slot_token].reshape(n_blocks, MOE_BLOCK, d)

    def expert_block(args):
        xblk, e = args
        gu = xblk @ w_gate_up[e] + b_gate_up[e]
        g, u = gu[..., ::2], gu[..., 1::2]
        g = jnp.minimum(g, SWIGLU_LIMIT)
        u = jnp.clip(u, -SWIGLU_LIMIT, SWIGLU_LIMIT)
        act = g * jax.nn.sigmoid(SWIGLU_ALPHA * g) * (u + 1.0)
        return act @ w_down[e] + b_down[e]

    yb = lax.map(expert_block, (xb, block_expert))
    contrib = yb.reshape(cap, d) * slot_gate[:, None].astype(yb.dtype)
    return jnp.zeros_like(h).at[slot_token].add(contrib.astype(h.dtype))


def trunk_layer(xc, xl, c, c_ctx, rope_diff, rope_gqa, p, layer_idx, with_ctx):
    mod_l = (jax.nn.silu(c) @ p['w_ada'] + p['b_ada'])[:, None, :]
    mod_c = (jax.nn.silu(c_ctx) @ p['w_ada'] + p['b_ada'])[None, None, :]
    sh1_l, sc1_l, g1_l, sh2_l, sc2_l, g2_l = jnp.split(mod_l, 6, axis=-1)
    sh1_c, sc1_c, g1_c, sh2_c, sc2_c, g2_c = jnp.split(mod_c, 6, axis=-1)

    uc = modulate(rms_norm(xc, p['norm1_w']), sh1_c, sc1_c) @ p['w_in']
    ul = modulate(rms_norm(xl, p['norm1_w']), sh1_l, sc1_l) @ p['w_in']
    cuts = [N_DIFF_IN, N_DIFF_IN + N_SSD_IN, N_DIFF_IN + N_SSD_IN + N_S5_IN]
    uc_a, uc_b, uc_c, uc_d = jnp.split(uc, cuts, axis=-1)
    ul_a, ul_b, ul_c, ul_d = jnp.split(ul, cuts, axis=-1)

    lam_init = 0.8 - 0.6 * math.exp(-0.3 * layer_idx)
    ca, la = diff_attention_mixer(uc_a, ul_a, rope_diff, p['diff_lq1'], p['diff_lk1'], p['diff_lq2'],
                                  p['diff_lk2'], p['diff_subln_w'], lam_init, with_ctx)
    cb, lb = ssd_mixer(uc_b, ul_b, p['ssd_conv_w'], p['ssd_conv_b'], p['ssd_dt_bias'], p['ssd_a_log'],
                       p['ssd_d'], p['ssd_norm_w'], with_ctx)
    cc, lc = s5_mixer(uc_c, ul_c, p['s5_lam_re'], p['s5_lam_im'], p['s5_log_dt'], p['s5_b_re'], p['s5_b_im'],
                      p['s5_c_re'], p['s5_c_im'], p['s5_d'], p['s5_w_glu'], p['s5_b_glu'], with_ctx)
    cd, ld = gqa_mixer(uc_d, ul_d, rope_gqa, p['gqa_q_norm_w'], p['gqa_k_norm_w'], with_ctx)

    xl = xl + g1_l * (jnp.concatenate([la, lb, lc, ld], axis=-1) @ p['w_out'])
    hl = modulate(rms_norm(xl, p['norm2_w']), sh2_l, sc2_l)
    moe_args = (p['moe_w_router'], p['moe_b_router'], p['moe_w_gate_up'], p['moe_b_gate_up'],
                p['moe_w_down'], p['moe_b_down'])
    if not with_ctx:
        y = moe_ffn(hl.reshape(-1, hl.shape[-1]), *moe_args)
        return xc, xl + g2_l * y.reshape(xl.shape)
    xc = xc + g1_c * (jnp.concatenate([ca, cb, cc, cd], axis=-1) @ p['w_out'])
    hc = modulate(rms_norm(xc, p['norm2_w']), sh2_c, sc2_c)
    n_ctx = hc.shape[0] * hc.shape[1]
    y = moe_ffn(jnp.concatenate([hc.reshape(-1, hc.shape[-1]), hl.reshape(-1, hl.shape[-1])], axis=0),
                *moe_args)
    return xc + g2_c * y[:n_ctx].reshape(xc.shape), xl + g2_l * y[n_ctx:].reshape(xl.shape)


def setup_inputs(seed: int = 0) -> dict:
    key = jax.random.key(seed)
    keys = iter([jax.random.fold_in(key, i) for i in range(64)])
    f32 = jnp.float32
    L = DEPTH

    def normal(shape, std):
        return std * jax.random.normal(next(keys), shape, f32)

    def gain(shape):
        return 1.0 + normal(shape, 0.02)

    def log_uniform(shape, lo, hi):
        return jax.random.uniform(next(keys), shape, f32, math.log(lo), math.log(hi))

    ssd_dt = jnp.exp(log_uniform((L, 2, SSD_HEADS), 1e-3, 1e-1))
    ssd_a = jax.random.uniform(next(keys), (L, 2, SSD_HEADS), f32, 1.0, 16.0)
    s5_n = jnp.arange(S5_STATE, dtype=f32)
    return {
        'x': normal((BATCH, SEQ, D_MODEL), 1.0),
        'c': normal((BATCH, D_MODEL), 1.0),
        'ctx': normal((BATCH, CTX_LEN, D_MODEL), 1.0),
        'c_ctx': normal((D_MODEL,), 1.0),
        'w_ada': normal((L, D_MODEL, 6 * D_MODEL), 0.5 * D_MODEL ** -0.5),
        'b_ada': normal((L, 6 * D_MODEL), 0.02),
        'norm1_w': gain((L, D_MODEL)),
        'norm2_w': gain((L, D_MODEL)),
        'w_in': normal((L, D_MODEL, D_IN), D_MODEL ** -0.5),
        'w_out': normal((L, D_MIX, D_MODEL), D_MIX ** -0.5),
        'diff_lq1': normal((L, DIFF_QK_DIM), 0.1),
        'diff_lk1': normal((L, DIFF_QK_DIM), 0.1),
        'diff_lq2': normal((L, DIFF_QK_DIM), 0.1),
        'diff_lk2': normal((L, DIFF_QK_DIM), 0.1),
        'diff_subln_w': gain((L, DIFF_V_DIM)),
        'ssd_conv_w': normal((L, SSD_CONV, SSD_XBC), SSD_CONV ** -0.5),
        'ssd_conv_b': normal((L, SSD_XBC), 0.02),
        'ssd_dt_bias': ssd_dt + jnp.log(-jnp.expm1(-ssd_dt)),
        'ssd_a_log': jnp.log(ssd_a),
        'ssd_d': gain((L, SSD_HEADS)),
        'ssd_norm_w': gain((L, W_SSD)),
        's5_lam_re': -0.5 + normal((L, 2, S5_GROUPS, S5_STATE), 0.01),
        's5_lam_im': jnp.pi * s5_n + normal((L, 2, S5_GROUPS, S5_STATE), 0.01),
        's5_log_dt': log_uniform((L, 2, S5_GROUPS), 1e-3, 1e-1),
        's5_b_re': normal((L, S5_GROUPS, S5_STATE, S5_GROUP), (2 * S5_GROUP) ** -0.5),
        's5_b_im': normal((L, S5_GROUPS, S5_STATE, S5_GROUP), (2 * S5_GROUP) ** -0.5),
        's5_c_re': normal((L, 2, S5_GROUPS, S5_GROUP, S5_STATE), S5_STATE ** -0.5),
        's5_c_im': normal((L, 2, S5_GROUPS, S5_GROUP, S5_STATE), S5_STATE ** -0.5),
        's5_d': gain((L, W_S5)),
        's5_w_glu': normal((L, W_S5, W_S5), W_S5 ** -0.5),
        's5_b_glu': normal((L, W_S5), 0.02),
        'gqa_q_norm_w': gain((L, GQA_HEAD_DIM)),
        'gqa_k_norm_w': gain((L, GQA_HEAD_DIM)),
        'moe_w_router': normal((L, D_MODEL, N_EXPERTS), D_MODEL ** -0.5),
        'moe_b_router': normal((L, N_EXPERTS), 0.01),
        'moe_w_gate_up': normal((L, N_EXPERTS, D_MODEL, 2 * D_EXPERT), D_MODEL ** -0.5),
        'moe_b_gate_up': normal((L, N_EXPERTS, 2 * D_EXPERT), 0.01),
        'moe_w_down': normal((L, N_EXPERTS, D_EXPERT, D_MODEL), D_EXPERT ** -0.5),
        'moe_b_down': normal((L, N_EXPERTS, D_MODEL), 0.01),
        'final_norm_w': gain((D_MODEL,)),
    }


def reference(x, c, ctx, c_ctx, w_ada, b_ada, norm1_w, norm2_w, w_in, w_out,
              diff_lq1, diff_lk1, diff_lq2, diff_lk2, diff_subln_w,
              ssd_conv_w, ssd_conv_b, ssd_dt_bias, ssd_a_log, ssd_d, ssd_norm_w,
              s5_lam_re, s5_lam_im, s5_log_dt, s5_b_re, s5_b_im, s5_c_re, s5_c_im, s5_d, s5_w_glu, s5_b_glu,
              gqa_q_norm_w, gqa_k_norm_w,
              moe_w_router, moe_b_router, moe_w_gate_up, moe_b_gate_up, moe_w_down, moe_b_down,
              final_norm_w):
    seq = x.shape[1]
    rows = seq // GRID_W
    row = jnp.repeat(jnp.arange(rows, dtype=jnp.int32), GRID_W)
    col = jnp.tile(jnp.arange(GRID_W, dtype=jnp.int32), rows)
    rope_diff = axial_rope_tables(row, col, DIFF_QK_DIM)
    rope_gqa = axial_rope_tables(row, col, GQA_HEAD_DIM)
    xc, xl = ctx, x
    for i in range(DEPTH):
        p = {
            'w_ada': w_ada[i], 'b_ada': b_ada[i], 'norm1_w': norm1_w[i], 'norm2_w': norm2_w[i],
            'w_in': w_in[i], 'w_out': w_out[i],
            'diff_lq1': diff_lq1[i], 'diff_lk1': diff_lk1[i], 'diff_lq2': diff_lq2[i], 'diff_lk2': diff_lk2[i],
            'diff_subln_w': diff_subln_w[i],
            'ssd_conv_w': ssd_conv_w[i], 'ssd_conv_b': ssd_conv_b[i], 'ssd_dt_bias': ssd_dt_bias[i],
            'ssd_a_log': ssd_a_log[i], 'ssd_d': ssd_d[i], 'ssd_norm_w': ssd_norm_w[i],
            's5_lam_re': s5_lam_re[i], 's5_lam_im': s5_lam_im[i], 's5_log_dt': s5_log_dt[i],
            's5_b_re': s5_b_re[i], 's5_b_im': s5_b_im[i], 's5_c_re': s5_c_re[i], 's5_c_im': s5_c_im[i],
            's5_d': s5_d[i], 's5_w_glu': s5_w_glu[i], 's5_b_glu': s5_b_glu[i],
            'gqa_q_norm_w': gqa_q_norm_w[i], 'gqa_k_norm_w': gqa_k_norm_w[i],
            'moe_w_router': moe_w_router[i], 'moe_b_router': moe_b_router[i],
            'moe_w_gate_up': moe_w_gate_up[i], 'moe_b_gate_up': moe_b_gate_up[i],
            'moe_w_down': moe_w_down[i], 'moe_b_down': moe_b_down[i],
        }
        xc, xl = trunk_layer(xc, xl, c, c_ctx, rope_diff, rope_gqa, p, i, i < DEPTH - 1)
    return rms_norm(xl, final_norm_w)
```

```python
import functools
import math

import numpy as np
import jax
import jax.numpy as jnp
from jax import lax
from jax.experimental import pallas as pl
from jax.experimental.pallas import tpu as pltpu

F32 = jnp.float32
BF16 = jnp.bfloat16
HIGHEST = lax.Precision.HIGHEST

GRID_W = 64
ROPE_BASE = 10000.0
NORM_EPS = 1e-6
LOG2E = math.log2(math.e)
DIFF_HEADS = 4
DIFF_QK_DIM = 32
DIFF_V_DIM = 64
SSD_HEADS = 4
SSD_HEAD_DIM = 64
SSD_GROUPS = 2
SSD_STATE = 128
SSD_CHUNK = 128
S5_GROUP = 16
S5_GROUPS = 16
S5_STATE = 64
S5_CHUNK = 16
GQA_HEADS = 4
GQA_KV_HEADS = 2
GQA_HEAD_DIM = 64
N_EXPERTS = 32
TOP_K = 4
SWIGLU_LIMIT = 7.0
SWIGLU_ALPHA = 1.702

LANES = 128
ROW_TILE = 256
MOE_BLOCK = 256
VMEM_LIMIT = 56 * 1024 * 1024

C_XBC, C_Z, C_QD, C_QDS, C_KD, C_KDS, C_VD = 0, 768, 1024, 1536, 2048, 2304, 2560
C_QG, C_QGS, C_KG, C_KGS, C_VG, C_DT, C_S5 = 2816, 3072, 3328, 3456, 3584, 3712, 3840
U_COLS = 4096


def _cparams(*sem):
    return pltpu.CompilerParams(dimension_semantics=sem, vmem_limit_bytes=VMEM_LIMIT)


def _rope_partner(d):
    j = np.arange(d)
    first = (j % (d // 2)) < (d // 4)
    return np.where(first, j + d // 4, j - d // 4), np.where(first, -1.0, 1.0)


def _in_proj_columns():
    src = np.full((U_COLS,), -1, np.int64)
    mul = np.zeros((U_COLS,), np.float32)

    def put(dst, cols, sign=None):
        cols = np.asarray(cols)
        src[dst:dst + cols.size] = cols
        mul[dst:dst + cols.size] = 1.0 if sign is None else sign

    pd, sd = _rope_partner(DIFF_QK_DIM)
    pg, sg = _rope_partner(GQA_HEAD_DIM)
    dd = np.arange(DIFF_QK_DIM)
    for h in range(DIFF_HEADS):
        for m in range(2):
            base = h * 64 + m * 32
            dst = h * 128 + (0 if m == 0 else 96)
            put(C_QD + dst, base + dd)
            put(C_QDS + dst, base + pd, sd)
            put(C_KD + base, 256 + base + dd)
            put(C_KDS + base, 256 + base + pd, sd)
    put(C_VD, 512 + np.arange(256))
    ssd0 = 768
    put(C_Z, ssd0 + np.arange(256))
    put(C_XBC, ssd0 + 256 + np.arange(768))
    put(C_DT, ssd0 + 1024 + np.arange(8))
    put(C_S5, 1800 + np.arange(256))
    g0 = 2056
    dg = np.arange(GQA_HEAD_DIM)
    for hq in range(GQA_HEADS):
        put(C_QG + hq * 64, g0 + hq * 64 + dg)
        put(C_QGS + hq * 64, g0 + hq * 64 + pg, sg)
    for hk in range(GQA_KV_HEADS):
        put(C_KG + hk * 64, g0 + 256 + hk * 64 + dg)
        put(C_KGS + hk * 64, g0 + 256 + hk * 64 + pg, sg)
    put(C_VG, g0 + 384 + np.arange(128))
    return src, mul


_IN_SRC, _IN_MUL = _in_proj_columns()


def _rope_tables(seq, n_ctx, d):
    rows = seq // GRID_W
    row = jnp.repeat(jnp.arange(rows, dtype=jnp.int32), GRID_W).astype(F32)
    col = jnp.tile(jnp.arange(GRID_W, dtype=jnp.int32), rows).astype(F32)
    axis_dim = d // 2
    inv_freq = ROPE_BASE ** (-jnp.arange(0, axis_dim, 2, dtype=F32) / axis_dim)
    j = np.arange(d)
    f = (j % axis_dim) % (d // 4)
    is_col = j >= axis_dim
    pos = jnp.where(jnp.asarray(is_col)[None, :], col[:, None], row[:, None])
    ang = pos * inv_freq[f][None, :]
    cos = jnp.tile(jnp.cos(ang), (1, LANES // d))
    sin = jnp.tile(jnp.sin(ang), (1, LANES // d))
    cos = jnp.concatenate([jnp.ones((n_ctx, LANES), F32), cos], axis=0)
    sin = jnp.concatenate([jnp.zeros((n_ctx, LANES), F32), sin], axis=0)
    return cos, sin


def _silu(x):
    return x * jax.nn.sigmoid(x)


def _ada_kernel(cc_ref, w_ref, b_ref, o_ref):
    s = _silu(cc_ref[...])
    o_ref[0] = jnp.dot(s, w_ref[0], precision=HIGHEST, preferred_element_type=F32) + b_ref[0]


def _ada_modulation(cc, w_ada, b_ada):
    depth, d, d6 = w_ada.shape
    return pl.pallas_call(
        _ada_kernel,
        out_shape=jax.ShapeDtypeStruct((depth, 8, d6), F32),
        grid=(depth, d6 // d),
        in_specs=[pl.BlockSpec((8, d), lambda l, j: (0, 0)),
                  pl.BlockSpec((1, d, d), lambda l, j: (l, 0, j)),
                  pl.BlockSpec((1, 1, d), lambda l, j: (l, 0, j))],
        out_specs=pl.BlockSpec((1, 8, d), lambda l, j: (l, 0, j)),
        compiler_params=_cparams("parallel", "parallel"),
        name="ada_modulation",
    )(cc, w_ada, b_ada.reshape(depth, 1, d6))


def _mod_rows(mod_ref, is_ctx, k, d):
    return jnp.where(is_ctx, mod_ref[0:1, k * d:(k + 1) * d], mod_ref[1:2, k * d:(k + 1) * d])


def _rms(x):
    return x * lax.rsqrt(jnp.mean(x * x, axis=-1, keepdims=True) + NORM_EPS)


def _in_proj_kernel(x_ref, mod_ref, nw_ref, w_ref, u_ref, *, ctx_tiles):
    d = x_ref.shape[1]
    is_ctx = pl.program_id(0) < ctx_tiles
    y = _rms(x_ref[...]) * nw_ref[...]
    h = y * (1.0 + _mod_rows(mod_ref, is_ctx, 1, d)) + _mod_rows(mod_ref, is_ctx, 0, d)
    u_ref[...] = jnp.dot(h.astype(BF16), w_ref[...], preferred_element_type=F32)


def _in_proj(x, mod, norm_w, w_ext, n_ctx):
    nt, d = x.shape
    return pl.pallas_call(
        functools.partial(_in_proj_kernel, ctx_tiles=n_ctx // ROW_TILE),
        out_shape=jax.ShapeDtypeStruct((nt, U_COLS), F32),
        grid=(nt // ROW_TILE,),
        in_specs=[pl.BlockSpec((ROW_TILE, d), lambda i: (i, 0)),
                  pl.BlockSpec(mod.shape, lambda i: (0, 0)),
                  pl.BlockSpec((1, d), lambda i: (0, 0)),
                  pl.BlockSpec((d, U_COLS), lambda i: (0, 0))],
        out_specs=pl.BlockSpec((ROW_TILE, U_COLS), lambda i: (i, 0)),
        compiler_params=_cparams("parallel"),
        name="in_proj",
    )(x, mod, norm_w.reshape(1, d), w_ext)


def _attn_prep_kernel(qd_ref, qds_ref, kd_ref, kds_ref, vd_ref, qg_ref, qgs_ref, kg_ref, kgs_ref, vg_ref,
                      cd_ref, sd_ref, cg_ref, sg_ref, nq_ref, nk_ref, seg_ref,
                      q_out, kt_out, v_out):
    cd, sd, cg, sg = cd_ref[...], sd_ref[...], cg_ref[...], sg_ref[...]
    diff_scale = DIFF_QK_DIM ** -0.5 * LOG2E
    gqa_scale = GQA_HEAD_DIM ** -0.5 * LOG2E
    ones_col = (lax.broadcasted_iota(jnp.int32, (qd_ref.shape[0], 64), 1) == 0).astype(BF16)

    def with_ones(v):
        return jnp.concatenate([v.astype(BF16), ones_col], axis=1)

    for h in range(DIFF_HEADS):
        sl = slice(h * LANES, (h + 1) * LANES)
        q = qd_ref[:, sl] * cd + qds_ref[:, sl] * sd
        q_out[h] = (q * diff_scale).astype(BF16)
        v_out[h] = with_ones(vd_ref[:, h * DIFF_V_DIM:(h + 1) * DIFF_V_DIM])
    for j in range(2):
        sl = slice(j * LANES, (j + 1) * LANES)
        k = kd_ref[:, sl] * cd + kds_ref[:, sl] * sd
        kt_out[sl, :] = k.T.astype(BF16)

    seg = seg_ref[...]

    def normed_rope(x, xs, w, ws):
        ms = jnp.dot(x * x, seg, precision=HIGHEST, preferred_element_type=F32)
        return (x * w * cg + xs * ws * sg) * lax.rsqrt(ms + NORM_EPS)

    for g in range(GQA_KV_HEADS):
        sl = slice(g * LANES, (g + 1) * LANES)
        q = normed_rope(qg_ref[:, sl], qgs_ref[:, sl], nq_ref[0:1, :], nq_ref[1:2, :])
        q_out[DIFF_HEADS + g] = (q * gqa_scale).astype(BF16)
        v_out[DIFF_HEADS + g] = with_ones(vg_ref[:, g * GQA_HEAD_DIM:(g + 1) * GQA_HEAD_DIM])
    k = normed_rope(kg_ref[...], kgs_ref[...], nk_ref[0:1, :], nk_ref[1:2, :])
    kt_out[2 * LANES:3 * LANES, :] = k.T.astype(BF16)


def _attn_prep(u, tables, q_norm_w, k_norm_w):
    nt = u.shape[0]
    t = ROW_TILE
    n_grp = DIFF_HEADS + GQA_KV_HEADS
    pg, _ = _rope_partner(GQA_HEAD_DIM)
    nq = jnp.stack([jnp.tile(q_norm_w, 2), jnp.tile(q_norm_w[pg], 2)])
    nk = jnp.stack([jnp.tile(k_norm_w, 2), jnp.tile(k_norm_w[pg], 2)])
    lane = np.arange(LANES)
    seg = jnp.asarray((lane[:, None] // GQA_HEAD_DIM == lane[None, :] // GQA_HEAD_DIM) / GQA_HEAD_DIM, F32)

    def ucol(width, c0):
        return pl.BlockSpec((t, width), lambda i, b=c0 // width: (i, b))

    tab = pl.BlockSpec((t, LANES), lambda i: (i, 0))
    small = lambda a: pl.BlockSpec(a.shape, lambda i: (0, 0))
    return pl.pallas_call(
        _attn_prep_kernel,
        out_shape=(jax.ShapeDtypeStruct((n_grp, nt, LANES), BF16),
                   jax.ShapeDtypeStruct((n_grp * 64, nt), BF16),
                   jax.ShapeDtypeStruct((n_grp, nt, LANES), BF16)),
        grid=(nt // t,),
        in_specs=[ucol(512, C_QD), ucol(512, C_QDS), ucol(256, C_KD), ucol(256, C_KDS), ucol(256, C_VD),
                  ucol(256, C_QG), ucol(256, C_QGS), ucol(128, C_KG), ucol(128, C_KGS), ucol(128, C_VG),
                  tab, tab, tab, tab, small(nq), small(nk), small(seg)],
        out_specs=(pl.BlockSpec((n_grp, t, LANES), lambda i: (0, i, 0)),
                   pl.BlockSpec((n_grp * 64, t), lambda i: (0, i)),
                   pl.BlockSpec((n_grp, t, LANES), lambda i: (0, i, 0))),
        compiler_params=_cparams("parallel"),
        name="attn_prep",
    )(u, u, u, u, u, u, u, u, u, u, *tables, nq, nk, seg)


def _attn_kernel(q_ref, kt_ref, v_ref, lq1_ref, lk1_ref, lq2_ref, lk2_ref, sub_ref, o_ref,
                 m_sc, acc_sc, sa_sc, sb_sc, *, n_ctx, kv_chunk, lam_init):
    g = pl.program_id(0)
    i = pl.program_id(1)
    tq = q_ref.shape[1]
    nt = kt_ref.shape[1]
    n_chunks = nt // kv_chunk
    q = q_ref[0]
    qs = jnp.concatenate([q[:, :64], q[:, 64:]], axis=0)

    m_sc[...] = jnp.full(m_sc.shape, -1e30, F32)
    acc_sc[...] = jnp.zeros(acc_sc.shape, F32)

    def scores(start, size):
        return jnp.dot(qs, kt_ref[:, pl.ds(start, size)], preferred_element_type=F32)

    def accumulate(s, start, size):
        m_prev = m_sc[...]
        m_new = jnp.maximum(m_prev, jnp.max(s, axis=-1, keepdims=True))
        p = jnp.exp2(s - m_new).astype(BF16)
        acc_sc[...] = jnp.exp2(m_prev - m_new) * acc_sc[...] + jnp.dot(
            p, v_ref[0, pl.ds(start, size), :], preferred_element_type=F32)
        m_sc[...] = m_new

    @pl.when(i < n_ctx // tq)
    def _():
        accumulate(scores(0, n_ctx), 0, n_ctx)

    @pl.when(i >= n_ctx // tq)
    def _():
        at = lambda c: pl.multiple_of(c * kv_chunk, kv_chunk)
        sa_sc[...] = scores(0, kv_chunk)

        def pair(pi, carry):
            c0 = 2 * pi
            sb_sc[...] = scores(at(c0 + 1), kv_chunk)
            accumulate(sa_sc[...], at(c0), kv_chunk)
            sa_sc[...] = scores(at(c0 + 2), kv_chunk)
            accumulate(sb_sc[...], at(c0 + 1), kv_chunk)
            return carry

        lax.fori_loop(0, (n_chunks - 1) // 2, pair, 0)
        if n_chunks % 2 == 0:
            last = (n_chunks - 1) * kv_chunk
            sb_sc[...] = scores(last, kv_chunk)
            accumulate(sa_sc[...], last - kv_chunk, kv_chunk)
            accumulate(sb_sc[...], last, kv_chunk)
        else:
            accumulate(sa_sc[...], (n_chunks - 1) * kv_chunk, kv_chunk)

    acc = acc_sc[...]
    out = acc[:, :64] / acc[:, 64:65]
    o1, o2 = out[:tq], out[tq:]

    @pl.when(g < DIFF_HEADS)
    def _():
        lam = (jnp.exp(jnp.sum(lq1_ref[...] * lk1_ref[...], axis=-1, keepdims=True))
               - jnp.exp(jnp.sum(lq2_ref[...] * lk2_ref[...], axis=-1, keepdims=True)) + lam_init)
        a = _rms(o1 - lam * o2) * sub_ref[...] * (1.0 - lam_init)
        o_ref[0] = jnp.concatenate([a, jnp.zeros_like(a)], axis=1)

    @pl.when(g >= DIFF_HEADS)
    def _():
        o_ref[0] = jnp.concatenate([o1, o2], axis=1)


def _pick_chunk(nt):
    for c in (1280, 1024, 640, 512, 256, 128):
        if nt % c == 0:
            return c
    raise ValueError(f"token count {nt} is not a multiple of 128")


def _attention(q_all, kt_all, v_all, lq1, lk1, lq2, lk2, subln_w, n_ctx, lam_init):
    n_grp, nt, _ = q_all.shape
    tq = ROW_TILE
    kv_chunk = _pick_chunk(nt)
    vec = lambda a: a.reshape(1, -1)
    small = pl.BlockSpec((1, DIFF_QK_DIM), lambda g, i: (0, 0))
    return pl.pallas_call(
        functools.partial(_attn_kernel, n_ctx=n_ctx, kv_chunk=kv_chunk, lam_init=lam_init),
        out_shape=jax.ShapeDtypeStruct((n_grp, nt, LANES), F32),
        grid=(n_grp, nt // tq),
        in_specs=[pl.BlockSpec((1, tq, LANES), lambda g, i: (g, i, 0)),
                  pl.BlockSpec((64, nt), lambda g, i: (g, 0)),
                  pl.BlockSpec((1, nt, LANES), lambda g, i: (g, 0, 0)),
                  small, small, small, small,
                  pl.BlockSpec((1, DIFF_V_DIM), lambda g, i: (0, 0))],
        out_specs=pl.BlockSpec((1, tq, LANES), lambda g, i: (g, i, 0)),
        scratch_shapes=[pltpu.VMEM((2 * tq, 1), F32), pltpu.VMEM((2 * tq, LANES), F32),
                        pltpu.VMEM((2 * tq, kv_chunk), F32), pltpu.VMEM((2 * tq, kv_chunk), F32)],
        compiler_params=_cparams("parallel", "parallel"),
        name="flash_attention",
    )(q_all, kt_all, v_all, vec(lq1), vec(lk1), vec(lq2), vec(lk2), vec(subln_w))


def _ssd_kernel(xf_ref, xfp_ref, xfn_ref, dtf_ref, xb_ref, xbp_ref, xbn_ref, dtb_ref,
                cw_ref, cb_ref, dtbias_ref, alog_ref, dskip_ref, yf_ref, yb_ref, stf_sc, stb_sc,
                *, ctx_chunks, n_chunks):
    j = pl.program_id(0)
    L = SSD_CHUNK
    cb = jnp.where(j < ctx_chunks, ctx_chunks - 1 - j, n_chunks + ctx_chunks - 1 - j)

    @pl.when(j == 0)
    def _():
        stf_sc[...] = jnp.zeros(stf_sc.shape, F32)
        stb_sc[...] = jnp.zeros(stb_sc.shape, F32)

    rows = lax.broadcasted_iota(jnp.int32, (L, 1), 0)
    r2 = lax.broadcasted_iota(jnp.int32, (L, L), 0)
    c2 = lax.broadcasted_iota(jnp.int32, (L, L), 1)
    a_row = -jnp.exp(alog_ref[...])

    def conv_silu(cur_ref, prev_ref, next_ref, c):
        has_prev = jnp.logical_and(c != 0, c != ctx_chunks)
        has_next = jnp.logical_and(c != ctx_chunks - 1, c != n_chunks - 1)
        x = cur_ref[...]
        prev_row = jnp.where(has_prev, prev_ref[7:8, :], 0.0)
        next_row = jnp.where(has_next, next_ref[0:1, :], 0.0)
        x_m1 = jnp.where(rows == 0, prev_row, pltpu.roll(x, 1, axis=0))
        x_p1 = jnp.where(rows == L - 1, next_row, pltpu.roll(x, L - 1, axis=0))
        y = x_m1 * cw_ref[0:1, :] + x * cw_ref[1:2, :] + x_p1 * cw_ref[2:3, :] + cb_ref[...]
        return _silu(y)

    def direction(xa, dt_raw, d, st_sc, y_ref):
        reverse = d == 1
        xs, bm, cm = xa[:, :256], xa[:, 256:512], xa[:, 512:768]
        dt = jax.nn.softplus(dt_raw + dtbias_ref[...])
        adt = dt * a_row
        mask = (c2 >= r2) if reverse else (c2 <= r2)
        cs = jnp.dot(mask.astype(F32), adt, precision=HIGHEST, preferred_element_type=F32)
        cs_t = cs.T
        tot = cs[0:1, :] if reverse else cs[L - 1:L, :]
        for g in range(SSD_GROUPS):
            bg = bm[:, g * SSD_STATE:(g + 1) * SSD_STATE]
            cg = cm[:, g * SSD_STATE:(g + 1) * SSD_STATE].astype(BF16)
            bg_t = bg.T.astype(BF16)
            gmat = jnp.dot(cg, bg_t, preferred_element_type=F32)
            for hh in range(SSD_HEADS // SSD_GROUPS):
                h = g * (SSD_HEADS // SSD_GROUPS) + hh
                lane = d * SSD_HEADS + h
                hs = slice(h * SSD_HEAD_DIM, (h + 1) * SSD_HEAD_DIM)
                cs_col = cs[:, lane:lane + 1]
                cs_row = cs_t[lane:lane + 1, :]
                lmat = jnp.exp(jnp.where(mask, cs_col - cs_row, -1e30))
                xh = xs[:, hs]
                xdt = xh * dt[:, lane:lane + 1]
                y = jnp.dot((gmat * lmat).astype(BF16), xdt.astype(BF16), preferred_element_type=F32)
                st = st_sc[h]
                y = y + jnp.dot(cg, st.astype(BF16), preferred_element_type=F32) * jnp.exp(cs_col)
                tot_h = tot[:, lane:lane + 1]
                dec = jnp.exp(tot_h - cs_col)
                st_sc[h] = jnp.exp(tot_h) * st + jnp.dot(bg_t, (xdt * dec).astype(BF16),
                                                         preferred_element_type=F32)
                if not reverse:
                    y = y + dskip_ref[:, hs] * xh
                y_ref[:, hs] = y

    direction(conv_silu(xf_ref, xfp_ref, xfn_ref, j), dtf_ref[...], 0, stf_sc, yf_ref)
    direction(conv_silu(xb_ref, xbp_ref, xbn_ref, cb), dtb_ref[...], 1, stb_sc, yb_ref)


def _ssd(u, conv_w, conv_b, dt_bias, a_log, d_skip, n_ctx):
    nt = u.shape[0]
    L = SSD_CHUNK
    n_chunks = nt // L
    ctx_chunks = n_ctx // L
    n_oct = nt // 8

    def cb_of(j):
        return jnp.where(j < ctx_chunks, ctx_chunks - 1 - j, n_chunks + ctx_chunks - 1 - j)

    fwd = lambda j: j
    pad8 = lambda v: jnp.pad(v.reshape(1, -1), ((0, 0), (0, LANES - v.size)))

    def specs(cf):
        return [pl.BlockSpec((L, 768), lambda j: (cf(j), C_XBC // 768)),
                pl.BlockSpec((8, 768), lambda j: (jnp.maximum(cf(j) * (L // 8) - 1, 0), C_XBC // 768)),
                pl.BlockSpec((8, 768), lambda j: (jnp.minimum((cf(j) + 1) * (L // 8), n_oct - 1), C_XBC // 768)),
                pl.BlockSpec((L, LANES), lambda j: (cf(j), C_DT // LANES))]

    small = lambda a: pl.BlockSpec(a.shape, lambda j: (0, 0))
    cbias = conv_b.reshape(1, -1)
    dtb = pad8(dt_bias)
    alog = pad8(a_log)
    dsk = jnp.repeat(d_skip, SSD_HEAD_DIM).reshape(1, -1)
    return pl.pallas_call(
        functools.partial(_ssd_kernel, ctx_chunks=ctx_chunks, n_chunks=n_chunks),
        out_shape=(jax.ShapeDtypeStruct((nt, 256), F32), jax.ShapeDtypeStruct((nt, 256), F32)),
        grid=(n_chunks,),
        in_specs=specs(fwd) + specs(cb_of) + [small(conv_w), small(cbias), small(dtb), small(alog), small(dsk)],
        out_specs=(pl.BlockSpec((L, 256), lambda j: (j, 0)),
                   pl.BlockSpec((L, 256), lambda j: (cb_of(j), 0))),
        scratch_shapes=[pltpu.VMEM((SSD_HEADS, SSD_STATE, SSD_HEAD_DIM), F32),
                        pltpu.VMEM((SSD_HEADS, SSD_STATE, SSD_HEAD_DIM), F32)],
        compiler_params=_cparams("arbitrary"),
        name="ssd_scan",
    )(u, u, u, u, u, u, u, u, conv_w, cbias, dtb, alog, dsk)


def _s5_matrices(lam_re, lam_im, log_dt, b_re, b_im, c_re, c_im, d_skip):
    T = S5_CHUNK
    lam = lax.complex(lam_re.astype(F32), lam_im.astype(F32))
    step = jnp.exp(log_dt.astype(F32))[..., None]
    a_bar = jnp.exp(lam * step)
    b_bar = ((a_bar - 1.0) / lam)[..., None] * lax.complex(b_re.astype(F32), b_im.astype(F32))[None]
    c_mat = lax.complex(c_re.astype(F32), c_im.astype(F32))
    k = jnp.arange(T + 1, dtype=F32)
    apow = jnp.exp((lam * step)[..., None] * k)
    kern = jnp.real(jnp.einsum('dgop,dgpt,dgpi->dgtoi', c_mat, apow[..., :T], b_bar, precision=HIGHEST))
    s_idx = np.arange(T)[:, None]
    t_idx = np.arange(T)[None, :]
    lag_f = np.clip(t_idx - s_idx, 0, T - 1)
    lag_b = np.clip(s_idx - t_idx, 0, T - 1)
    kf = jnp.where(jnp.asarray(s_idx <= t_idx)[None, :, :, None, None], kern[0][:, lag_f], 0.0)
    kb = jnp.where(jnp.asarray(s_idx >= t_idx)[None, :, :, None, None], kern[1][:, lag_b], 0.0)
    toep = jnp.transpose(kf + kb, (0, 1, 4, 2, 3)).reshape(S5_GROUPS, T * S5_GROUP, T * S5_GROUP)
    pf = apow[0][..., T - 1 - np.arange(T)]
    pb = apow[1][..., np.arange(T)]
    bsf = jnp.einsum('gps,gpi->gsip', pf, b_bar[0]).reshape(S5_GROUPS, T * S5_GROUP, S5_STATE)
    bsb = jnp.einsum('gps,gpi->gsip', pb, b_bar[1]).reshape(S5_GROUPS, T * S5_GROUP, S5_STATE)

    def ri(z):
        return jnp.concatenate([jnp.real(z), jnp.imag(z)], -1), jnp.concatenate([jnp.imag(z), jnp.real(z)], -1)

    bsf_n, bsf_s = ri(bsf)
    bsb_n, bsb_s = ri(bsb)
    w_a = jnp.concatenate([toep, bsf_n, bsf_s, bsb_n, bsb_s], axis=-1)
    cf = jnp.einsum('gop,gpt->gpto', c_mat[0], apow[0][..., 1 + np.arange(T)]).reshape(S5_GROUPS, S5_STATE, -1)
    cb = jnp.einsum('gop,gpt->gpto', c_mat[1], apow[1][..., T - np.arange(T)]).reshape(S5_GROUPS, S5_STATE, -1)
    w_c = jnp.concatenate([jnp.real(cf), -jnp.imag(cf), jnp.real(cb), -jnp.imag(cb)], axis=1)
    a16 = apow[..., T]
    ar, ai = jnp.real(a16), jnp.imag(a16)
    zeros = jnp.zeros_like(ar[0])
    coef = jnp.stack([jnp.concatenate([ar[0], ar[0]], -1), jnp.concatenate([-ai[0], ai[0]], -1),
                      jnp.concatenate([ai[0], -ai[0]], -1),
                      jnp.concatenate([ar[1], ar[1]], -1), jnp.concatenate([-ai[1], ai[1]], -1),
                      jnp.concatenate([ai[1], -ai[1]], -1),
                      jnp.concatenate([zeros, zeros], -1), jnp.concatenate([zeros, zeros], -1)], axis=1)
    d_row = jnp.tile(d_skip.astype(F32).reshape(S5_GROUPS, 1, S5_GROUP), (1, 1, T))
    return w_a.astype(BF16), w_c.astype(BF16), coef, d_row


def _s5_kernel(u_ref, wa_ref, wc_ref, coef_ref, d_ref, y_ref, s_sc, h_sc, *, ctx_chunks):
    n_chunks = u_ref.shape[1]
    u = u_ref[0]
    sa = jnp.dot(u.astype(BF16), wa_ref[0], preferred_element_type=F32)
    y_ref[0] = sa[:, :256] + d_ref[0] * u
    s_sc[...] = sa[:, 256:]
    coef = coef_ref[0]
    a1f, a2f, a2sf = coef[0:1], coef[1:2], coef[2:3]
    a1b, a2b, a2sb = coef[3:4], coef[4:5], coef[5:6]

    n_oct = n_chunks // 8
    ctx_oct = ctx_chunks // 8
    rows = lax.broadcasted_iota(jnp.int32, (8, 1), 0)

    def body(kb, carry):
        hf, hfs, hb, hbs = carry
        cb = jnp.where(kb < ctx_oct, ctx_oct - 1 - kb, n_oct + ctx_oct - 1 - kb)
        f0 = pl.multiple_of(kb * 8, 8)
        b0 = pl.multiple_of(cb * 8, 8)
        sf = s_sc[pl.ds(f0, 8), 0:256]
        sb = s_sc[pl.ds(b0, 8), 256:512]
        hf_blk = jnp.zeros((8, 128), F32)
        hb_blk = jnp.zeros((8, 128), F32)
        for r in range(8):
            rb = 7 - r
            hf_blk = jnp.where(rows == r, hf, hf_blk)
            hb_blk = jnp.where(rows == rb, hb, hb_blk)
            hf, hfs = (a1f * hf + a2f * hfs + sf[r:r + 1, 0:128],
                       a1f * hfs + a2sf * hf + sf[r:r + 1, 128:256])
            hb, hbs = (a1b * hb + a2b * hbs + sb[rb:rb + 1, 0:128],
                       a1b * hbs + a2sb * hb + sb[rb:rb + 1, 128:256])
        h_sc[pl.ds(f0, 8), 0:128] = hf_blk
        h_sc[pl.ds(b0, 8), 128:256] = hb_blk
        return hf, hfs, hb, hbs

    z = jnp.zeros((1, 128), F32)
    lax.fori_loop(0, n_oct, body, (z, z, z, z))
    y_ref[0] += jnp.dot(h_sc[...].astype(BF16), wc_ref[0], preferred_element_type=F32)


def _s5(u_s5, mats, n_ctx):
    nt = u_s5.shape[0]
    T = S5_CHUNK
    nc = nt // T
    w_a, w_c, coef, d_row = mats
    ug = u_s5.reshape(nc, T, S5_GROUPS, S5_GROUP).transpose(2, 0, 1, 3).reshape(S5_GROUPS, nc, T * S5_GROUP)
    per_g = lambda a: pl.BlockSpec((1,) + a.shape[1:], lambda g: (g, 0, 0))
    y = pl.pallas_call(
        functools.partial(_s5_kernel, ctx_chunks=n_ctx // T),
        out_shape=jax.ShapeDtypeStruct(ug.shape, F32),
        grid=(S5_GROUPS,),
        in_specs=[per_g(ug), per_g(w_a), per_g(w_c), per_g(coef), per_g(d_row)],
        out_specs=per_g(ug),
        scratch_shapes=[pltpu.VMEM((nc, 512), F32), pltpu.VMEM((nc, 256), F32)],
        compiler_params=_cparams("parallel"),
        name="s5_scan",
    )(ug, w_a, w_c, coef, d_row)
    return y.reshape(S5_GROUPS, nc, T, S5_GROUP).transpose(1, 2, 0, 3).reshape(nt, S5_GROUPS * S5_GROUP)


def _out_proj_kernel(x_ref, o_ref, yf_ref, yb_ref, z_ref, y5_ref, mod_ref, ssdn_ref, wglu_ref, bglu_ref,
                     wo_ref, n2_ref, wr_ref, br_ref, xn_ref, h_ref, lg_ref, *, ctx_tiles):
    d = x_ref.shape[1]
    is_ctx = pl.program_id(0) < ctx_tiles
    lb = _rms((yf_ref[...] + yb_ref[...]) * _silu(z_ref[...])) * ssdn_ref[...]
    y5 = jax.nn.gelu(y5_ref[...])
    lc = y5 * jax.nn.sigmoid(jnp.dot(y5.astype(BF16), wglu_ref[...], preferred_element_type=F32) + bglu_ref[...])
    mix = jnp.concatenate([o_ref[k] for k in range(o_ref.shape[0])] + [lb, lc], axis=1).astype(BF16)
    proj = jnp.dot(mix, wo_ref[...], preferred_element_type=F32)
    xn = x_ref[...] + _mod_rows(mod_ref, is_ctx, 2, d) * proj
    xn_ref[...] = xn
    h = _rms(xn) * n2_ref[...] * (1.0 + _mod_rows(mod_ref, is_ctx, 4, d)) + _mod_rows(mod_ref, is_ctx, 3, d)
    h_ref[...] = h.astype(BF16)
    lg_ref[...] = jnp.dot(h, wr_ref[...], precision=HIGHEST, preferred_element_type=F32) + br_ref[...]


def _out_proj(x, o_attn, yf, yb, u, y5, mod, ssd_norm_w, w_glu, b_glu, w_out_ext, norm2_w, w_router, b_router, n_ctx):
    nt, d = x.shape
    t = ROW_TILE
    n_grp = o_attn.shape[0]
    row = lambda w: pl.BlockSpec((t, w), lambda i: (i, 0))
    small = lambda a: pl.BlockSpec(a.shape, lambda i: (0, 0))
    wr = jnp.pad(w_router, ((0, 0), (0, LANES - w_router.shape[1])))
    br = jnp.pad(b_router.reshape(1, -1), ((0, 0), (0, LANES - b_router.shape[0])))
    args = (mod, ssd_norm_w.reshape(1, -1), w_glu.astype(BF16), b_glu.reshape(1, -1), w_out_ext,
            norm2_w.reshape(1, -1), wr, br)
    return pl.pallas_call(
        functools.partial(_out_proj_kernel, ctx_tiles=n_ctx // t),
        out_shape=(jax.ShapeDtypeStruct((nt, d), F32), jax.ShapeDtypeStruct((nt, d), BF16),
                   jax.ShapeDtypeStruct((nt, LANES), F32)),
        grid=(nt // t,),
        in_specs=[row(d), pl.BlockSpec((n_grp, t, LANES), lambda i: (0, i, 0)), row(256), row(256),
                  pl.BlockSpec((t, 256), lambda i: (i, C_Z // 256)), row(256)] + [small(a) for a in args],
        out_specs=(row(d), row(d), row(LANES)),
        compiler_params=_cparams("parallel"),
        name="out_proj",
    )(x, o_attn, yf, yb, u, y5, *args)


def _moe_kernel(be_ref, nb_ref, x_ref, wgu_ref, bgu_ref, wd_ref, bd_ref, y_ref):
    b = pl.program_id(0)

    @pl.when(b < nb_ref[0])
    def _():
        gu = jnp.dot(x_ref[...], wgu_ref[0], preferred_element_type=F32) + bgu_ref[0]
        up = pltpu.roll(gu, gu.shape[1] - 1, axis=1)
        g = jnp.minimum(gu, SWIGLU_LIMIT)
        u = jnp.clip(up, -SWIGLU_LIMIT, SWIGLU_LIMIT)
        act = g * jax.nn.sigmoid(SWIGLU_ALPHA * g) * (u + 1.0)
        y_ref[...] = jnp.dot(act.astype(BF16), wd_ref[0], preferred_element_type=F32) + bd_ref[0]

    @pl.when(b >= nb_ref[0])
    def _():
        y_ref[...] = jnp.zeros(y_ref.shape, F32)


def _moe_experts(xs, block_expert, n_used, wgu, bgu, wd2, bd):
    cap, d = xs.shape
    de2 = wgu.shape[2]
    n_blocks = cap // MOE_BLOCK
    wspec = lambda k, n: pl.BlockSpec((1, k, n), lambda b, be, nb: (be[b], 0, 0))
    grid_spec = pltpu.PrefetchScalarGridSpec(
        num_scalar_prefetch=2,
        grid=(n_blocks,),
        in_specs=[pl.BlockSpec((MOE_BLOCK, d), lambda b, be, nb: (b, 0)),
                  wspec(d, de2), wspec(1, de2), wspec(de2, d), wspec(1, d)],
        out_specs=pl.BlockSpec((MOE_BLOCK, d), lambda b, be, nb: (b, 0)),
    )
    return pl.pallas_call(
        _moe_kernel,
        out_shape=jax.ShapeDtypeStruct((cap, d), F32),
        grid_spec=grid_spec,
        compiler_params=_cparams("arbitrary"),
        name="moe_experts",
    )(block_expert, n_used, xs, wgu, bgu, wd2, bd)


def _route(logits, n_tokens):
    top_val, top_idx = lax.top_k(logits[:, :N_EXPERTS], TOP_K)
    gates = jax.nn.softmax(top_val, axis=-1)
    flat = top_idx.reshape(-1)
    m = flat.shape[0]
    onehot = (flat[:, None] == jnp.arange(N_EXPERTS, dtype=flat.dtype)[None, :]).astype(jnp.int32)
    csum = jnp.cumsum(onehot, axis=0)
    rank = jnp.take_along_axis(csum, flat[:, None], axis=1)[:, 0] - 1
    counts = csum[-1]
    padded = (counts + MOE_BLOCK - 1) // MOE_BLOCK * MOE_BLOCK
    pend = jnp.cumsum(padded)
    pstart = pend - padded
    dest = pstart[flat] + rank
    n_blocks = -(-(m + N_EXPERTS * (MOE_BLOCK - 1)) // MOE_BLOCK)
    block_expert = jnp.minimum(
        jnp.searchsorted(pend, jnp.arange(n_blocks, dtype=jnp.int32) * MOE_BLOCK, side='right'),
        N_EXPERTS - 1).astype(jnp.int32)
    n_used = (pend[-1] // MOE_BLOCK).astype(jnp.int32).reshape(1)
    return gates, dest.astype(jnp.int32), block_expert, n_used, n_blocks


def _final_norm_kernel(x_ref, w_ref, o_ref):
    o_ref[...] = _rms(x_ref[...]) * w_ref[...]


def _final_norm(x, w, n_ctx):
    nt, d = x.shape
    t = ROW_TILE
    off = n_ctx // t
    return pl.pallas_call(
        _final_norm_kernel,
        out_shape=jax.ShapeDtypeStruct((nt - n_ctx, d), F32),
        grid=((nt - n_ctx) // t,),
        in_specs=[pl.BlockSpec((t, d), lambda i: (i + off, 0)), pl.BlockSpec((1, d), lambda i: (0, 0))],
        out_specs=pl.BlockSpec((t, d), lambda i: (i, 0)),
        compiler_params=_cparams("parallel"),
        name="final_norm",
    )(x, w.reshape(1, d))


def kernel(x, c, ctx, c_ctx, w_ada, b_ada, norm1_w, norm2_w, w_in, w_out, diff_lq1, diff_lk1, diff_lq2, diff_lk2, diff_subln_w, ssd_conv_w, ssd_conv_b, ssd_dt_bias, ssd_a_log, ssd_d, ssd_norm_w, s5_lam_re, s5_lam_im, s5_log_dt, s5_b_re, s5_b_im, s5_c_re, s5_c_im, s5_d, s5_w_glu, s5_b_glu, gqa_q_norm_w, gqa_k_norm_w, moe_w_router, moe_b_router, moe_w_gate_up, moe_b_gate_up, moe_w_down, moe_b_down, final_norm_w):
    batch, seq, d = x.shape
    assert batch == 1, "kernels are written for a single sequence"
    n_ctx = ctx.shape[1]
    depth = w_in.shape[0]
    nt = n_ctx + seq
    assert n_ctx % ROW_TILE == 0 and seq % ROW_TILE == 0 and seq % GRID_W == 0

    xt = jnp.concatenate([ctx[0], x[0]], axis=0)
    cc = jnp.zeros((8, d), F32).at[0].set(c_ctx).at[1].set(c[0])
    mod_all = _ada_modulation(cc, w_ada, b_ada)
    tables = _rope_tables(seq, n_ctx, DIFF_QK_DIM) + _rope_tables(seq, n_ctx, GQA_HEAD_DIM)

    src = jnp.asarray(np.maximum(_IN_SRC, 0))
    mul = jnp.asarray(_IN_MUL)
    wo_rows = np.full((6 * LANES + 512,), -1, np.int64)
    for h in range(DIFF_HEADS):
        wo_rows[h * LANES:h * LANES + 64] = h * 64 + np.arange(64)
    wo_rows[4 * LANES:6 * LANES] = 768 + np.arange(256)
    wo_rows[6 * LANES:6 * LANES + 512] = 256 + np.arange(512)
    wo_src = jnp.asarray(np.maximum(wo_rows, 0))
    wo_mul = jnp.asarray((wo_rows >= 0).astype(np.float32))

    for l in range(depth):
        lam_init = 0.8 - 0.6 * math.exp(-0.3 * l)
        mod = mod_all[l]
        w_ext = (w_in[l][:, src] * mul[None, :]).astype(BF16)
        w_out_ext = (w_out[l][wo_src, :] * wo_mul[:, None]).astype(BF16)

        u = _in_proj(xt, mod, norm1_w[l], w_ext, n_ctx)
        q_all, kt_all, v_all = _attn_prep(u, tables, gqa_q_norm_w[l], gqa_k_norm_w[l])
        o_attn = _attention(q_all, kt_all, v_all, diff_lq1[l], diff_lk1[l], diff_lq2[l], diff_lk2[l],
                            diff_subln_w[l], n_ctx, lam_init)
        yf, yb = _ssd(u, ssd_conv_w[l], ssd_conv_b[l], ssd_dt_bias[l], ssd_a_log[l], ssd_d[l], n_ctx)
        mats = _s5_matrices(s5_lam_re[l], s5_lam_im[l], s5_log_dt[l], s5_b_re[l], s5_b_im[l],
                            s5_c_re[l], s5_c_im[l], s5_d[l])
        y5 = _s5(u[:, C_S5:C_S5 + 256], mats, n_ctx)
        xt, h, logits = _out_proj(xt, o_attn, yf, yb, u, y5, mod, ssd_norm_w[l], s5_w_glu[l], s5_b_glu[l],
                                  w_out_ext, norm2_w[l], moe_w_router[l], moe_b_router[l], n_ctx)

        gates, dest, block_expert, n_used, n_blocks = _route(logits, nt)
        xs = jnp.zeros((n_blocks * MOE_BLOCK, d), BF16).at[dest].set(jnp.repeat(h, TOP_K, axis=0))
        wd = moe_w_down[l].astype(BF16)
        wd2 = jnp.stack([wd, jnp.zeros_like(wd)], axis=2).reshape(wd.shape[0], 2 * wd.shape[1], wd.shape[2])
        yb_moe = _moe_experts(xs, block_expert, n_used, moe_w_gate_up[l].astype(BF16),
                              moe_b_gate_up[l][:, None, :], wd2, moe_b_down[l][:, None, :])
        y = jnp.sum(yb_moe[dest].reshape(nt, TOP_K, d) * gates[:, :, None], axis=1)
        g2 = jnp.concatenate([jnp.broadcast_to(mod[0:1, 5 * d:], (n_ctx, d)),
                              jnp.broadcast_to(mod[1:2, 5 * d:], (seq, d))], axis=0)
        xt = xt + g2 * y

    return _final_norm(xt, final_norm_w, n_ctx)[None]
```

```python
import functools
import math

import numpy as np
import jax
import jax.numpy as jnp
from jax import lax
from jax.experimental import pallas as pl
from jax.experimental.pallas import tpu as pltpu

F32 = jnp.float32
BF16 = jnp.bfloat16
HIGHEST = lax.Precision.HIGHEST

GRID_W = 64
ROPE_BASE = 10000.0
NORM_EPS = 1e-6
LOG2E = math.log2(math.e)
DIFF_HEADS = 4
DIFF_QK_DIM = 32
DIFF_V_DIM = 64
SSD_HEADS = 4
SSD_HEAD_DIM = 64
SSD_GROUPS = 2
SSD_STATE = 128
SSD_CHUNK = 128
S5_GROUP = 16
S5_GROUPS = 16
S5_STATE = 64
S5_CHUNK = 16
GQA_HEADS = 4
GQA_KV_HEADS = 2
GQA_HEAD_DIM = 64
N_EXPERTS = 32
TOP_K = 4
SWIGLU_LIMIT = 7.0
SWIGLU_ALPHA = 1.702

LANES = 128
ROW_TILE = 256
MOE_BLOCK = 256
VMEM_LIMIT = 56 * 1024 * 1024

C_XBC, C_Z, C_QD, C_QDS, C_KD, C_KDS, C_VD = 0, 768, 1024, 1536, 2048, 2304, 2560
C_QG, C_QGS, C_KG, C_KGS, C_VG, C_DT, C_S5 = 2816, 3072, 3328, 3456, 3584, 3712, 3840
U_COLS = 4096


def _cparams(*sem):
    return pltpu.CompilerParams(dimension_semantics=sem, vmem_limit_bytes=VMEM_LIMIT)


def _rope_partner(d):
    j = np.arange(d)
    first = (j % (d // 2)) < (d // 4)
    return np.where(first, j + d // 4, j - d // 4), np.where(first, -1.0, 1.0)


def _in_proj_columns():
    src = np.full((U_COLS,), -1, np.int64)
    mul = np.zeros((U_COLS,), np.float32)

    def put(dst, cols, sign=None):
        cols = np.asarray(cols)
        src[dst:dst + cols.size] = cols
        mul[dst:dst + cols.size] = 1.0 if sign is None else sign

    pd, sd = _rope_partner(DIFF_QK_DIM)
    pg, sg = _rope_partner(GQA_HEAD_DIM)
    dd = np.arange(DIFF_QK_DIM)
    for h in range(DIFF_HEADS):
        for m in range(2):
            base = h * 64 + m * 32
            dst = h * 128 + (0 if m == 0 else 96)
            put(C_QD + dst, base + dd)
            put(C_QDS + dst, base + pd, sd)
            put(C_KD + base, 256 + base + dd)
            put(C_KDS + base, 256 + base + pd, sd)
    put(C_VD, 512 + np.arange(256))
    ssd0 = 768
    put(C_Z, ssd0 + np.arange(256))
    put(C_XBC, ssd0 + 256 + np.arange(768))
    put(C_DT, ssd0 + 1024 + np.arange(8))
    put(C_S5, 1800 + np.arange(256))
    g0 = 2056
    dg = np.arange(GQA_HEAD_DIM)
    for hq in range(GQA_HEADS):
        put(C_QG + hq * 64, g0 + hq * 64 + dg)
        put(C_QGS + hq * 64, g0 + hq * 64 + pg, sg)
    for hk in range(GQA_KV_HEADS):
        put(C_KG + hk * 64, g0 + 256 + hk * 64 + dg)
        put(C_KGS + hk * 64, g0 + 256 + hk * 64 + pg, sg)
    put(C_VG, g0 + 384 + np.arange(128))
    return src, mul


_IN_SRC, _IN_MUL = _in_proj_columns()


def _rope_tables(seq, n_ctx, d):
    rows = seq // GRID_W
    row = jnp.repeat(jnp.arange(rows, dtype=jnp.int32), GRID_W).astype(F32)
    col = jnp.tile(jnp.arange(GRID_W, dtype=jnp.int32), rows).astype(F32)
    axis_dim = d // 2
    inv_freq = ROPE_BASE ** (-jnp.arange(0, axis_dim, 2, dtype=F32) / axis_dim)
    j = np.arange(d)
    f = (j % axis_dim) % (d // 4)
    is_col = j >= axis_dim
    pos = jnp.where(jnp.asarray(is_col)[None, :], col[:, None], row[:, None])
    ang = pos * inv_freq[f][None, :]
    cos = jnp.tile(jnp.cos(ang), (1, LANES // d))
    sin = jnp.tile(jnp.sin(ang), (1, LANES // d))
    cos = jnp.concatenate([jnp.ones((n_ctx, LANES), F32), cos], axis=0)
    sin = jnp.concatenate([jnp.zeros((n_ctx, LANES), F32), sin], axis=0)
    return cos, sin


def _silu(x):
    return x * jax.nn.sigmoid(x)


def _ada_kernel(cc_ref, w_ref, b_ref, o_ref):
    s = _silu(cc_ref[...])
    o_ref[0] = jnp.dot(s, w_ref[0], precision=HIGHEST, preferred_element_type=F32) + b_ref[0]


def _ada_modulation(cc, w_ada, b_ada):
    depth, d, d6 = w_ada.shape
    return pl.pallas_call(
        _ada_kernel,
        out_shape=jax.ShapeDtypeStruct((depth, 8, d6), F32),
        grid=(depth, d6 // d),
        in_specs=[pl.BlockSpec((8, d), lambda l, j: (0, 0)),
                  pl.BlockSpec((1, d, d), lambda l, j: (l, 0, j)),
                  pl.BlockSpec((1, 1, d), lambda l, j: (l, 0, j))],
        out_specs=pl.BlockSpec((1, 8, d), lambda l, j: (l, 0, j)),
        compiler_params=_cparams("parallel", "parallel"),
        name="ada_modulation",
    )(cc, w_ada, b_ada.reshape(depth, 1, d6))


def _mod_rows(mod_ref, is_ctx, k, d):
    return jnp.where(is_ctx, mod_ref[0:1, k * d:(k + 1) * d], mod_ref[1:2, k * d:(k + 1) * d])


def _rms(x):
    return x * lax.rsqrt(jnp.mean(x * x, axis=-1, keepdims=True) + NORM_EPS)


def _in_proj_kernel(x_ref, mod_ref, nw_ref, w_ref, u_ref, *, ctx_tiles):
    d = x_ref.shape[1]
    is_ctx = pl.program_id(0) < ctx_tiles
    y = _rms(x_ref[...]) * nw_ref[...]
    h = y * (1.0 + _mod_rows(mod_ref, is_ctx, 1, d)) + _mod_rows(mod_ref, is_ctx, 0, d)
    u_ref[...] = jnp.dot(h.astype(BF16), w_ref[...], preferred_element_type=F32)


def _in_proj(x, mod, norm_w, w_ext, n_ctx):
    nt, d = x.shape
    return pl.pallas_call(
        functools.partial(_in_proj_kernel, ctx_tiles=n_ctx // ROW_TILE),
        out_shape=jax.ShapeDtypeStruct((nt, U_COLS), F32),
        grid=(nt // ROW_TILE,),
        in_specs=[pl.BlockSpec((ROW_TILE, d), lambda i: (i, 0)),
                  pl.BlockSpec(mod.shape, lambda i: (0, 0)),
                  pl.BlockSpec((1, d), lambda i: (0, 0)),
                  pl.BlockSpec((d, U_COLS), lambda i: (0, 0))],
        out_specs=pl.BlockSpec((ROW_TILE, U_COLS), lambda i: (i, 0)),
        compiler_params=_cparams("parallel"),
        name="in_proj",
    )(x, mod, norm_w.reshape(1, d), w_ext)


def _attn_prep_kernel(qd_ref, qds_ref, kd_ref, kds_ref, vd_ref, qg_ref, qgs_ref, kg_ref, kgs_ref, vg_ref,
                      cd_ref, sd_ref, cg_ref, sg_ref, nq_ref, nk_ref, seg_ref,
                      q_out, kt_out, v_out):
    cd, sd, cg, sg = cd_ref[...], sd_ref[...], cg_ref[...], sg_ref[...]
    diff_scale = DIFF_QK_DIM ** -0.5 * LOG2E
    gqa_scale = GQA_HEAD_DIM ** -0.5 * LOG2E
    ones_col = (lax.broadcasted_iota(jnp.int32, (qd_ref.shape[0], 64), 1) == 0).astype(BF16)

    def with_ones(v):
        return jnp.concatenate([v.astype(BF16), ones_col], axis=1)

    for h in range(DIFF_HEADS):
        sl = slice(h * LANES, (h + 1) * LANES)
        q = qd_ref[:, sl] * cd + qds_ref[:, sl] * sd
        q_out[h] = (q * diff_scale).astype(BF16)
        v_out[h] = with_ones(vd_ref[:, h * DIFF_V_DIM:(h + 1) * DIFF_V_DIM])
    for j in range(2):
        sl = slice(j * LANES, (j + 1) * LANES)
        k = kd_ref[:, sl] * cd + kds_ref[:, sl] * sd
        kt_out[sl, :] = k.T.astype(BF16)

    seg = seg_ref[...]

    def normed_rope(x, xs, w, ws):
        ms = jnp.dot(x * x, seg, precision=HIGHEST, preferred_element_type=F32)
        return (x * w * cg + xs * ws * sg) * lax.rsqrt(ms + NORM_EPS)

    for g in range(GQA_KV_HEADS):
        sl = slice(g * LANES, (g + 1) * LANES)
        q = normed_rope(qg_ref[:, sl], qgs_ref[:, sl], nq_ref[0:1, :], nq_ref[1:2, :])
        q_out[DIFF_HEADS + g] = (q * gqa_scale).astype(BF16)
        v_out[DIFF_HEADS + g] = with_ones(vg_ref[:, g * GQA_HEAD_DIM:(g + 1) * GQA_HEAD_DIM])
    k = normed_rope(kg_ref[...], kgs_ref[...], nk_ref[0:1, :], nk_ref[1:2, :])
    kt_out[2 * LANES:3 * LANES, :] = k.T.astype(BF16)


def _attn_prep(u, tables, q_norm_w, k_norm_w):
    nt = u.shape[0]
    t = ROW_TILE
    n_grp = DIFF_HEADS + GQA_KV_HEADS
    pg, _ = _rope_partner(GQA_HEAD_DIM)
    nq = jnp.stack([jnp.tile(q_norm_w, 2), jnp.tile(q_norm_w[pg], 2)])
    nk = jnp.stack([jnp.tile(k_norm_w, 2), jnp.tile(k_norm_w[pg], 2)])
    lane = np.arange(LANES)
    seg = jnp.asarray((lane[:, None] // GQA_HEAD_DIM == lane[None, :] // GQA_HEAD_DIM) / GQA_HEAD_DIM, F32)

    def ucol(width, c0):
        return pl.BlockSpec((t, width), lambda i, b=c0 // width: (i, b))

    tab = pl.BlockSpec((t, LANES), lambda i: (i, 0))
    small = lambda a: pl.BlockSpec(a.shape, lambda i: (0, 0))
    return pl.pallas_call(
        _attn_prep_kernel,
        out_shape=(jax.ShapeDtypeStruct((n_grp, nt, LANES), BF16),
                   jax.ShapeDtypeStruct((n_grp * 64, nt), BF16),
                   jax.ShapeDtypeStruct((n_grp, nt, LANES), BF16)),
        grid=(nt // t,),
        in_specs=[ucol(512, C_QD), ucol(512, C_QDS), ucol(256, C_KD), ucol(256, C_KDS), ucol(256, C_VD),
                  ucol(256, C_QG), ucol(256, C_QGS), ucol(128, C_KG), ucol(128, C_KGS), ucol(128, C_VG),
                  tab, tab, tab, tab, small(nq), small(nk), small(seg)],
        out_specs=(pl.BlockSpec((n_grp, t, LANES), lambda i: (0, i, 0)),
                   pl.BlockSpec((n_grp * 64, t), lambda i: (0, i)),
                   pl.BlockSpec((n_grp, t, LANES), lambda i: (0, i, 0))),
        compiler_params=_cparams("parallel"),
        name="attn_prep",
    )(u, u, u, u, u, u, u, u, u, u, *tables, nq, nk, seg)


def _attn_kernel(q_ref, kt_ref, v_ref, lq1_ref, lk1_ref, lq2_ref, lk2_ref, sub_ref, o_ref,
                 m_sc, acc_sc, sa_sc, sb_sc, *, n_ctx, kv_chunk, lam_init):
    g = pl.program_id(0)
    i = pl.program_id(1)
    tq = q_ref.shape[1]
    nt = kt_ref.shape[1]
    n_chunks = nt // kv_chunk
    q = q_ref[0]
    qs = jnp.concatenate([q[:, :64], q[:, 64:]], axis=0)

    m_sc[...] = jnp.full(m_sc.shape, -1e30, F32)
    acc_sc[...] = jnp.zeros(acc_sc.shape, F32)

    def scores(start, size):
        return jnp.dot(qs, kt_ref[:, pl.ds(start, size)], preferred_element_type=F32)

    def accumulate(s, start, size):
        m_prev = m_sc[...]
        m_new = jnp.maximum(m_prev, jnp.max(s, axis=-1, keepdims=True))
        p = jnp.exp2(s - m_new).astype(BF16)
        acc_sc[...] = jnp.exp2(m_prev - m_new) * acc_sc[...] + jnp.dot(
            p, v_ref[0, pl.ds(start, size), :], preferred_element_type=F32)
        m_sc[...] = m_new

    @pl.when(i < n_ctx // tq)
    def _():
        accumulate(scores(0, n_ctx), 0, n_ctx)

    @pl.when(i >= n_ctx // tq)
    def _():
        at = lambda c: pl.multiple_of(c * kv_chunk, kv_chunk)
        sa_sc[...] = scores(0, kv_chunk)

        def pair(pi, carry):
            c0 = 2 * pi
            sb_sc[...] = scores(at(c0 + 1), kv_chunk)
            accumulate(sa_sc[...], at(c0), kv_chunk)
            sa_sc[...] = scores(at(c0 + 2), kv_chunk)
            accumulate(sb_sc[...], at(c0 + 1), kv_chunk)
            return carry

        lax.fori_loop(0, (n_chunks - 1) // 2, pair, 0)
        if n_chunks % 2 == 0:
            last = (n_chunks - 1) * kv_chunk
            sb_sc[...] = scores(last, kv_chunk)
            accumulate(sa_sc[...], last - kv_chunk, kv_chunk)
            accumulate(sb_sc[...], last, kv_chunk)
        else:
            accumulate(sa_sc[...], (n_chunks - 1) * kv_chunk, kv_chunk)

    acc = acc_sc[...]
    out = acc[:, :64] / acc[:, 64:65]
    o1, o2 = out[:tq], out[tq:]

    @pl.when(g < DIFF_HEADS)
    def _():
        lam = (jnp.exp(jnp.sum(lq1_ref[...] * lk1_ref[...], axis=-1, keepdims=True))
               - jnp.exp(jnp.sum(lq2_ref[...] * lk2_ref[...], axis=-1, keepdims=True)) + lam_init)
        a = _rms(o1 - lam * o2) * sub_ref[...] * (1.0 - lam_init)
        o_ref[0] = jnp.concatenate([a, jnp.zeros_like(a)], axis=1)

    @pl.when(g >= DIFF_HEADS)
    def _():
        o_ref[0] = jnp.concatenate([o1, o2], axis=1)


def _pick_chunk(nt):
    for c in (1280, 1024, 640, 512, 256, 128):
        if nt % c == 0:
            return c
    raise ValueError(f"token count {nt} is not a multiple of 128")


def _attention(q_all, kt_all, v_all, lq1, lk1, lq2, lk2, subln_w, n_ctx, lam_init):
    n_grp, nt, _ = q_all.shape
    tq = ROW_TILE
    kv_chunk = _pick_chunk(nt)
    vec = lambda a: a.reshape(1, -1)
    small = pl.BlockSpec((1, DIFF_QK_DIM), lambda g, i: (0, 0))
    return pl.pallas_call(
        functools.partial(_attn_kernel, n_ctx=n_ctx, kv_chunk=kv_chunk, lam_init=lam_init),
        out_shape=jax.ShapeDtypeStruct((n_grp, nt, LANES), F32),
        grid=(n_grp, nt // tq),
        in_specs=[pl.BlockSpec((1, tq, LANES), lambda g, i: (g, i, 0)),
                  pl.BlockSpec((64, nt), lambda g, i: (g, 0)),
                  pl.BlockSpec((1, nt, LANES), lambda g, i: (g, 0, 0)),
                  small, small, small, small,
                  pl.BlockSpec((1, DIFF_V_DIM), lambda g, i: (0, 0))],
        out_specs=pl.BlockSpec((1, tq, LANES), lambda g, i: (g, i, 0)),
        scratch_shapes=[pltpu.VMEM((2 * tq, 1), F32), pltpu.VMEM((2 * tq, LANES), F32),
                        pltpu.VMEM((2 * tq, kv_chunk), F32), pltpu.VMEM((2 * tq, kv_chunk), F32)],
        compiler_params=_cparams("parallel", "parallel"),
        name="flash_attention",
    )(q_all, kt_all, v_all, vec(lq1), vec(lk1), vec(lq2), vec(lk2), vec(subln_w))


def _ssd_kernel(xf_ref, xfp_ref, xfn_ref, dtf_ref, xb_ref, xbp_ref, xbn_ref, dtb_ref,
                cw_ref, cb_ref, dtbias_ref, alog_ref, dskip_ref, yf_ref, yb_ref, stf_sc, stb_sc,
                *, ctx_chunks, n_chunks):
    j = pl.program_id(0)
    L = SSD_CHUNK
    cb = jnp.where(j < ctx_chunks, ctx_chunks - 1 - j, n_chunks + ctx_chunks - 1 - j)

    @pl.when(j == 0)
    def _():
        stf_sc[...] = jnp.zeros(stf_sc.shape, F32)
        stb_sc[...] = jnp.zeros(stb_sc.shape, F32)

    rows = lax.broadcasted_iota(jnp.int32, (L, 1), 0)
    r2 = lax.broadcasted_iota(jnp.int32, (L, L), 0)
    c2 = lax.broadcasted_iota(jnp.int32, (L, L), 1)
    a_row = -jnp.exp(alog_ref[...])

    def conv_silu(cur_ref, prev_ref, next_ref, c):
        has_prev = jnp.logical_and(c != 0, c != ctx_chunks)
        has_next = jnp.logical_and(c != ctx_chunks - 1, c != n_chunks - 1)
        x = cur_ref[...]
        prev_row = jnp.where(has_prev, prev_ref[7:8, :], 0.0)
        next_row = jnp.where(has_next, next_ref[0:1, :], 0.0)
        x_m1 = jnp.where(rows == 0, prev_row, pltpu.roll(x, 1, axis=0))
        x_p1 = jnp.where(rows == L - 1, next_row, pltpu.roll(x, L - 1, axis=0))
        y = x_m1 * cw_ref[0:1, :] + x * cw_ref[1:2, :] + x_p1 * cw_ref[2:3, :] + cb_ref[...]
        return _silu(y)

    def direction(xa, dt_raw, d, st_sc, y_ref):
        reverse = d == 1
        xs, bm, cm = xa[:, :256], xa[:, 256:512], xa[:, 512:768]
        dt = jax.nn.softplus(dt_raw + dtbias_ref[...])
        adt = dt * a_row
        mask = (c2 >= r2) if reverse else (c2 <= r2)
        cs = jnp.dot(mask.astype(F32), adt, precision=HIGHEST, preferred_element_type=F32)
        cs_t = cs.T
        tot = cs[0:1, :] if reverse else cs[L - 1:L, :]
        for g in range(SSD_GROUPS):
            bg = bm[:, g * SSD_STATE:(g + 1) * SSD_STATE]
            cg = cm[:, g * SSD_STATE:(g + 1) * SSD_STATE].astype(BF16)
            bg_t = bg.T.astype(BF16)
            gmat = jnp.dot(cg, bg_t, preferred_element_type=F32)
            for hh in range(SSD_HEADS // SSD_GROUPS):
                h = g * (SSD_HEADS // SSD_GROUPS) + hh
                lane = d * SSD_HEADS + h
                hs = slice(h * SSD_HEAD_DIM, (h + 1) * SSD_HEAD_DIM)
                cs_col = cs[:, lane:lane + 1]
                cs_row = cs_t[lane:lane + 1, :]
                lmat = jnp.exp(jnp.where(mask, cs_col - cs_row, -1e30))
                xh = xs[:, hs]
                xdt = xh * dt[:, lane:lane + 1]
                y = jnp.dot((gmat * lmat).astype(BF16), xdt.astype(BF16), preferred_element_type=F32)
                st = st_sc[h]
                y = y + jnp.dot(cg, st.astype(BF16), preferred_element_type=F32) * jnp.exp(cs_col)
                tot_h = tot[:, lane:lane + 1]
                dec = jnp.exp(tot_h - cs_col)
                st_sc[h] = jnp.exp(tot_h) * st + jnp.dot(bg_t, (xdt * dec).astype(BF16),
                                                         preferred_element_type=F32)
                if not reverse:
                    y = y + dskip_ref[:, hs] * xh
                y_ref[:, hs] = y

    direction(conv_silu(xf_ref, xfp_ref, xfn_ref, j), dtf_ref[...], 0, stf_sc, yf_ref)
    direction(conv_silu(xb_ref, xbp_ref, xbn_ref, cb), dtb_ref[...], 1, stb_sc, yb_ref)


def _ssd(u, conv_w, conv_b, dt_bias, a_log, d_skip, n_ctx):
    nt = u.shape[0]
    L = SSD_CHUNK
    n_chunks = nt // L
    ctx_chunks = n_ctx // L
    n_oct = nt // 8

    def cb_of(j):
        return jnp.where(j < ctx_chunks, ctx_chunks - 1 - j, n_chunks + ctx_chunks - 1 - j)

    fwd = lambda j: j
    pad8 = lambda v: jnp.pad(v.reshape(1, -1), ((0, 0), (0, LANES - v.size)))

    def specs(cf):
        return [pl.BlockSpec((L, 768), lambda j: (cf(j), C_XBC // 768)),
                pl.BlockSpec((8, 768), lambda j: (jnp.maximum(cf(j) * (L // 8) - 1, 0), C_XBC // 768)),
                pl.BlockSpec((8, 768), lambda j: (jnp.minimum((cf(j) + 1) * (L // 8), n_oct - 1), C_XBC // 768)),
                pl.BlockSpec((L, LANES), lambda j: (cf(j), C_DT // LANES))]

    small = lambda a: pl.BlockSpec(a.shape, lambda j: (0, 0))
    cbias = conv_b.reshape(1, -1)
    dtb = pad8(dt_bias)
    alog = pad8(a_log)
    dsk = jnp.repeat(d_skip, SSD_HEAD_DIM).reshape(1, -1)
    return pl.pallas_call(
        functools.partial(_ssd_kernel, ctx_chunks=ctx_chunks, n_chunks=n_chunks),
        out_shape=(jax.ShapeDtypeStruct((nt, 256), F32), jax.ShapeDtypeStruct((nt, 256), F32)),
        grid=(n_chunks,),
        in_specs=specs(fwd) + specs(cb_of) + [small(conv_w), small(cbias), small(dtb), small(alog), small(dsk)],
        out_specs=(pl.BlockSpec((L, 256), lambda j: (j, 0)),
                   pl.BlockSpec((L, 256), lambda j: (cb_of(j), 0))),
        scratch_shapes=[pltpu.VMEM((SSD_HEADS, SSD_STATE, SSD_HEAD_DIM), F32),
                        pltpu.VMEM((SSD_HEADS, SSD_STATE, SSD_HEAD_DIM), F32)],
        compiler_params=_cparams("arbitrary"),
        name="ssd_scan",
    )(u, u, u, u, u, u, u, u, conv_w, cbias, dtb, alog, dsk)


def _s5_matrices(lam_re, lam_im, log_dt, b_re, b_im, c_re, c_im, d_skip):
    T = S5_CHUNK
    lam = lax.complex(lam_re.astype(F32), lam_im.astype(F32))
    step = jnp.exp(log_dt.astype(F32))[..., None]
    a_bar = jnp.exp(lam * step)
    b_bar = ((a_bar - 1.0) / lam)[..., None] * lax.complex(b_re.astype(F32), b_im.astype(F32))[None]
    c_mat = lax.complex(c_re.astype(F32), c_im.astype(F32))
    k = jnp.arange(T + 1, dtype=F32)
    apow = jnp.exp((lam * step)[..., None] * k)
    kern = jnp.real(jnp.einsum('dgop,dgpt,dgpi->dgtoi', c_mat, apow[..., :T], b_bar, precision=HIGHEST))
    s_idx = np.arange(T)[:, None]
    t_idx = np.arange(T)[None, :]
    lag_f = np.clip(t_idx - s_idx, 0, T - 1)
    lag_b = np.clip(s_idx - t_idx, 0, T - 1)
    kf = jnp.where(jnp.asarray(s_idx <= t_idx)[None, :, :, None, None], kern[0][:, lag_f], 0.0)
    kb = jnp.where(jnp.asarray(s_idx >= t_idx)[None, :, :, None, None], kern[1][:, lag_b], 0.0)
    toep = jnp.transpose(kf + kb, (0, 1, 4, 2, 3)).reshape(S5_GROUPS, T * S5_GROUP, T * S5_GROUP)
    pf = apow[0][..., T - 1 - np.arange(T)]
    pb = apow[1][..., np.arange(T)]
    bsf = jnp.einsum('gps,gpi->gsip', pf, b_bar[0]).reshape(S5_GROUPS, T * S5_GROUP, S5_STATE)
    bsb = jnp.einsum('gps,gpi->gsip', pb, b_bar[1]).reshape(S5_GROUPS, T * S5_GROUP, S5_STATE)

    def ri(z):
        return jnp.concatenate([jnp.real(z), jnp.imag(z)], -1), jnp.concatenate([jnp.imag(z), jnp.real(z)], -1)

    bsf_n, bsf_s = ri(bsf)
    bsb_n, bsb_s = ri(bsb)
    w_a = jnp.concatenate([toep, bsf_n, bsf_s, bsb_n, bsb_s], axis=-1)
    cf = jnp.einsum('gop,gpt->gpto', c_mat[0], apow[0][..., 1 + np.arange(T)]).reshape(S5_GROUPS, S5_STATE, -1)
    cb = jnp.einsum('gop,gpt->gpto', c_mat[1], apow[1][..., T - np.arange(T)]).reshape(S5_GROUPS, S5_STATE, -1)
    w_c = jnp.concatenate([jnp.real(cf), -jnp.imag(cf), jnp.real(cb), -jnp.imag(cb)], axis=1)
    a16 = apow[..., T]
    ar, ai = jnp.real(a16), jnp.imag(a16)
    zeros = jnp.zeros_like(ar[0])
    coef = jnp.stack([jnp.concatenate([ar[0], ar[0]], -1), jnp.concatenate([-ai[0], ai[0]], -1),
                      jnp.concatenate([ai[0], -ai[0]], -1),
                      jnp.concatenate([ar[1], ar[1]], -1), jnp.concatenate([-ai[1], ai[1]], -1),
                      jnp.concatenate([ai[1], -ai[1]], -1),
                      jnp.concatenate([zeros, zeros], -1), jnp.concatenate([zeros, zeros], -1)], axis=1)
    d_row = jnp.tile(d_skip.astype(F32).reshape(S5_GROUPS, 1, S5_GROUP), (1, 1, T))
    return w_a.astype(BF16), w_c.astype(BF16), coef, d_row


def _s5_kernel(u_ref, wa_ref, wc_ref, coef_ref, d_ref, y_ref, s_sc, h_sc, *, ctx_chunks):
    n_chunks = u_ref.shape[1]
    u = u_ref[0]
    sa = jnp.dot(u.astype(BF16), wa_ref[0], preferred_element_type=F32)
    y_ref[0] = sa[:, :256] + d_ref[0] * u
    s_sc[...] = sa[:, 256:]
    coef = coef_ref[0]
    a1f, a2f, a2sf = coef[0:1], coef[1:2], coef[2:3]
    a1b, a2b, a2sb = coef[3:4], coef[4:5], coef[5:6]

    n_oct = n_chunks // 8
    ctx_oct = ctx_chunks // 8
    rows = lax.broadcasted_iota(jnp.int32, (8, 1), 0)

    def body(kb, carry):
        hf, hfs, hb, hbs = carry
        cb = jnp.where(kb < ctx_oct, ctx_oct - 1 - kb, n_oct + ctx_oct - 1 - kb)
        f0 = pl.multiple_of(kb * 8, 8)
        b0 = pl.multiple_of(cb * 8, 8)
        sf = s_sc[pl.ds(f0, 8), 0:256]
        sb = s_sc[pl.ds(b0, 8), 256:512]
        hf_blk = jnp.zeros((8, 128), F32)
        hb_blk = jnp.zeros((8, 128), F32)
        for r in range(8):
            rb = 7 - r
            hf_blk = jnp.where(rows == r, hf, hf_blk)
            hb_blk = jnp.where(rows == rb, hb, hb_blk)
            hf, hfs = (a1f * hf + a2f * hfs + sf[r:r + 1, 0:128],
                       a1f * hfs + a2sf * hf + sf[r:r + 1, 128:256])
            hb, hbs = (a1b * hb + a2b * hbs + sb[rb:rb + 1, 0:128],
                       a1b * hbs + a2sb * hb + sb[rb:rb + 1, 128:256])
        h_sc[pl.ds(f0, 8), 0:128] = hf_blk
        h_sc[pl.ds(b0, 8), 128:256] = hb_blk
        return hf, hfs, hb, hbs

    z = jnp.zeros((1, 128), F32)
    lax.fori_loop(0, n_oct, body, (z, z, z, z))
    y_ref[0] += jnp.dot(h_sc[...].astype(BF16), wc_ref[0], preferred_element_type=F32)


def _s5(u_s5, mats, n_ctx):
    nt = u_s5.shape[0]
    T = S5_CHUNK
    nc = nt // T
    w_a, w_c, coef, d_row = mats
    ug = u_s5.reshape(nc, T, S5_GROUPS, S5_GROUP).transpose(2, 0, 1, 3).reshape(S5_GROUPS, nc, T * S5_GROUP)
    per_g = lambda a: pl.BlockSpec((1,) + a.shape[1:], lambda g: (g, 0, 0))
    y = pl.pallas_call(
        functools.partial(_s5_kernel, ctx_chunks=n_ctx // T),
        out_shape=jax.ShapeDtypeStruct(ug.shape, F32),
        grid=(S5_GROUPS,),
        in_specs=[per_g(ug), per_g(w_a), per_g(w_c), per_g(coef), per_g(d_row)],
        out_specs=per_g(ug),
        scratch_shapes=[pltpu.VMEM((nc, 512), F32), pltpu.VMEM((nc, 256), F32)],
        compiler_params=_cparams("parallel"),
        name="s5_scan",
    )(ug, w_a, w_c, coef, d_row)
    return y.reshape(S5_GROUPS, nc, T, S5_GROUP).transpose(1, 2, 0, 3).reshape(nt, S5_GROUPS * S5_GROUP)


def _out_proj_kernel(x_ref, o_ref, yf_ref, yb_ref, z_ref, y5_ref, mod_ref, ssdn_ref, wglu_ref, bglu_ref,
                     wo_ref, n2_ref, wr_ref, br_ref, xn_ref, h_ref, lg_ref, *, ctx_tiles):
    d = x_ref.shape[1]
    is_ctx = pl.program_id(0) < ctx_tiles
    lb = _rms((yf_ref[...] + yb_ref[...]) * _silu(z_ref[...])) * ssdn_ref[...]
    y5 = jax.nn.gelu(y5_ref[...])
    lc = y5 * jax.nn.sigmoid(jnp.dot(y5.astype(BF16), wglu_ref[...], preferred_element_type=F32) + bglu_ref[...])
    mix = jnp.concatenate([o_ref[k] for k in range(o_ref.shape[0])] + [lb, lc], axis=1).astype(BF16)
    proj = jnp.dot(mix, wo_ref[...], preferred_element_type=F32)
    xn = x_ref[...] + _mod_rows(mod_ref, is_ctx, 2, d) * proj
    xn_ref[...] = xn
    h = _rms(xn) * n2_ref[...] * (1.0 + _mod_rows(mod_ref, is_ctx, 4, d)) + _mod_rows(mod_ref, is_ctx, 3, d)
    h_ref[...] = h
    lg_ref[...] = jnp.dot(h, wr_ref[...], precision=HIGHEST, preferred_element_type=F32) + br_ref[...]


def _out_proj(x, o_attn, yf, yb, u, y5, mod, ssd_norm_w, w_glu, b_glu, w_out_ext, norm2_w, w_router, b_router, n_ctx):
    nt, d = x.shape
    t = ROW_TILE
    n_grp = o_attn.shape[0]
    row = lambda w: pl.BlockSpec((t, w), lambda i: (i, 0))
    small = lambda a: pl.BlockSpec(a.shape, lambda i: (0, 0))
    wr = jnp.pad(w_router, ((0, 0), (0, LANES - w_router.shape[1])))
    br = jnp.pad(b_router.reshape(1, -1), ((0, 0), (0, LANES - b_router.shape[0])))
    args = (mod, ssd_norm_w.reshape(1, -1), w_glu.astype(BF16), b_glu.reshape(1, -1), w_out_ext,
            norm2_w.reshape(1, -1), wr, br)
    return pl.pallas_call(
        functools.partial(_out_proj_kernel, ctx_tiles=n_ctx // t),
        out_shape=(jax.ShapeDtypeStruct((nt, d), F32), jax.ShapeDtypeStruct((nt, d), F32),
                   jax.ShapeDtypeStruct((nt, LANES), F32)),
        grid=(nt // t,),
        in_specs=[row(d), pl.BlockSpec((n_grp, t, LANES), lambda i: (0, i, 0)), row(256), row(256),
                  pl.BlockSpec((t, 256), lambda i: (i, C_Z // 256)), row(256)] + [small(a) for a in args],
        out_specs=(row(d), row(d), row(LANES)),
        compiler_params=_cparams("parallel"),
        name="out_proj",
    )(x, o_attn, yf, yb, u, y5, *args)


def _moe_kernel(be_ref, nb_ref, x_ref, wgu_ref, bgu_ref, wd_ref, bd_ref, y_ref, wgu_sc, wd2_sc):
    b = pl.program_id(0)
    live = b < nb_ref[0]

    @pl.when(jnp.logical_and(live, jnp.logical_or(b == 0, be_ref[b] != be_ref[jnp.maximum(b - 1, 0)])))
    def _():
        wgu_sc[...] = wgu_ref[0, 0].astype(BF16)
        hi = pltpu.bitcast(wd_ref[0, 0].astype(BF16).astype(F32), jnp.uint32) & jnp.uint32(0xFFFF0000)
        wd2_sc[...] = pltpu.bitcast(hi | (hi >> 16), BF16)

    @pl.when(live)
    def _():
        gu = jnp.dot(x_ref[...].astype(BF16), wgu_sc[...], preferred_element_type=F32) + bgu_ref[0, 0]
        up = pltpu.roll(gu, gu.shape[1] - 1, axis=1)
        g = jnp.minimum(gu, SWIGLU_LIMIT)
        u = jnp.clip(up, -SWIGLU_LIMIT, SWIGLU_LIMIT)
        even = lax.broadcasted_iota(jnp.int32, gu.shape, 1) % 2 == 0
        act = jnp.where(even, g * jax.nn.sigmoid(SWIGLU_ALPHA * g) * (u + 1.0), 0.0)
        y_ref[...] = jnp.dot(act.astype(BF16), wd2_sc[...], preferred_element_type=F32) + bd_ref[0, 0]

    @pl.when(jnp.logical_not(live))
    def _():
        y_ref[...] = jnp.zeros(y_ref.shape, F32)


def _moe_experts(xs, block_expert, n_used, layer, wgu, bgu, wd, bd):
    cap, d = xs.shape
    depth, n_exp, _, de2 = wgu.shape
    n_blocks = cap // MOE_BLOCK
    bgu = bgu.reshape(depth, n_exp, 1, de2)
    bd = bd.reshape(depth, n_exp, 1, d)
    wspec = lambda k, n: pl.BlockSpec((1, 1, k, n), lambda b, be, nb: (layer, be[b], 0, 0))
    grid_spec = pltpu.PrefetchScalarGridSpec(
        num_scalar_prefetch=2,
        grid=(n_blocks,),
        in_specs=[pl.BlockSpec((MOE_BLOCK, d), lambda b, be, nb: (b, 0)),
                  wspec(d, de2), wspec(1, de2), wspec(de2 // 2, d), wspec(1, d)],
        out_specs=pl.BlockSpec((MOE_BLOCK, d), lambda b, be, nb: (b, 0)),
        scratch_shapes=[pltpu.VMEM((d, de2), BF16), pltpu.VMEM((de2, d), BF16)],
    )
    return pl.pallas_call(
        _moe_kernel,
        out_shape=jax.ShapeDtypeStruct((cap, d), F32),
        grid_spec=grid_spec,
        compiler_params=_cparams("arbitrary"),
        name="moe_experts",
    )(block_expert, n_used, xs, wgu, bgu, wd, bd)


def _route(logits, n_tokens):
    top_val, top_idx = lax.top_k(logits[:, :N_EXPERTS], TOP_K)
    gates = jax.nn.softmax(top_val, axis=-1)
    flat = top_idx.reshape(-1)
    m = flat.shape[0]
    onehot = (flat[:, None] == jnp.arange(N_EXPERTS, dtype=flat.dtype)[None, :]).astype(jnp.int32)
    csum = jnp.cumsum(onehot, axis=0)
    rank = jnp.take_along_axis(csum, flat[:, None], axis=1)[:, 0] - 1
    counts = csum[-1]
    padded = (counts + MOE_BLOCK - 1) // MOE_BLOCK * MOE_BLOCK
    pend = jnp.cumsum(padded)
    pstart = pend - padded
    dest = pstart[flat] + rank
    n_blocks = -(-(m + N_EXPERTS * (MOE_BLOCK - 1)) // MOE_BLOCK)
    block_start = jnp.arange(n_blocks, dtype=jnp.int32) * MOE_BLOCK
    block_expert = jnp.minimum(jnp.sum((pend[None, :] <= block_start[:, None]).astype(jnp.int32), axis=1),
                               N_EXPERTS - 1)
    n_used = (pend[-1] // MOE_BLOCK).astype(jnp.int32).reshape(1)
    return gates, dest.astype(jnp.int32), block_expert, n_used, n_blocks


def _final_norm_kernel(x_ref, w_ref, o_ref):
    o_ref[...] = _rms(x_ref[...]) * w_ref[...]


def _final_norm(x, w, n_ctx):
    nt, d = x.shape
    t = ROW_TILE
    off = n_ctx // t
    return pl.pallas_call(
        _final_norm_kernel,
        out_shape=jax.ShapeDtypeStruct((nt - n_ctx, d), F32),
        grid=((nt - n_ctx) // t,),
        in_specs=[pl.BlockSpec((t, d), lambda i: (i + off, 0)), pl.BlockSpec((1, d), lambda i: (0, 0))],
        out_specs=pl.BlockSpec((t, d), lambda i: (i, 0)),
        compiler_params=_cparams("parallel"),
        name="final_norm",
    )(x, w.reshape(1, d))


def kernel(x, c, ctx, c_ctx, w_ada, b_ada, norm1_w, norm2_w, w_in, w_out, diff_lq1, diff_lk1, diff_lq2, diff_lk2, diff_subln_w, ssd_conv_w, ssd_conv_b, ssd_dt_bias, ssd_a_log, ssd_d, ssd_norm_w, s5_lam_re, s5_lam_im, s5_log_dt, s5_b_re, s5_b_im, s5_c_re, s5_c_im, s5_d, s5_w_glu, s5_b_glu, gqa_q_norm_w, gqa_k_norm_w, moe_w_router, moe_b_router, moe_w_gate_up, moe_b_gate_up, moe_w_down, moe_b_down, final_norm_w):
    batch, seq, d = x.shape
    assert batch == 1, "kernels are written for a single sequence"
    n_ctx = ctx.shape[1]
    depth = w_in.shape[0]
    nt = n_ctx + seq
    assert n_ctx % ROW_TILE == 0 and seq % ROW_TILE == 0 and seq % GRID_W == 0

    xt = jnp.concatenate([ctx[0], x[0]], axis=0)
    cc = jnp.zeros((8, d), F32).at[0].set(c_ctx).at[1].set(c[0])
    mod_all = _ada_modulation(cc, w_ada, b_ada)
    tables = _rope_tables(seq, n_ctx, DIFF_QK_DIM) + _rope_tables(seq, n_ctx, GQA_HEAD_DIM)

    src = jnp.asarray(np.maximum(_IN_SRC, 0))
    mul = jnp.asarray(_IN_MUL)
    wo_rows = np.full((6 * LANES + 512,), -1, np.int64)
    for h in range(DIFF_HEADS):
        wo_rows[h * LANES:h * LANES + 64] = h * 64 + np.arange(64)
    wo_rows[4 * LANES:6 * LANES] = 768 + np.arange(256)
    wo_rows[6 * LANES:6 * LANES + 512] = 256 + np.arange(512)
    wo_src = jnp.asarray(np.maximum(wo_rows, 0))
    wo_mul = jnp.asarray((wo_rows >= 0).astype(np.float32))

    for l in range(depth):
        lam_init = 0.8 - 0.6 * math.exp(-0.3 * l)
        mod = mod_all[l]
        w_ext = (w_in[l][:, src] * mul[None, :]).astype(BF16)
        w_out_ext = (w_out[l][wo_src, :] * wo_mul[:, None]).astype(BF16)

        u = _in_proj(xt, mod, norm1_w[l], w_ext, n_ctx)
        q_all, kt_all, v_all = _attn_prep(u, tables, gqa_q_norm_w[l], gqa_k_norm_w[l])
        o_attn = _attention(q_all, kt_all, v_all, diff_lq1[l], diff_lk1[l], diff_lq2[l], diff_lk2[l],
                            diff_subln_w[l], n_ctx, lam_init)
        yf, yb = _ssd(u, ssd_conv_w[l], ssd_conv_b[l], ssd_dt_bias[l], ssd_a_log[l], ssd_d[l], n_ctx)
        mats = _s5_matrices(s5_lam_re[l], s5_lam_im[l], s5_log_dt[l], s5_b_re[l], s5_b_im[l],
                            s5_c_re[l], s5_c_im[l], s5_d[l])
        y5 = _s5(u[:, C_S5:C_S5 + 256], mats, n_ctx)
        xt, h, logits = _out_proj(xt, o_attn, yf, yb, u, y5, mod, ssd_norm_w[l], s5_w_glu[l], s5_b_glu[l],
                                  w_out_ext, norm2_w[l], moe_w_router[l], moe_b_router[l], n_ctx)

        gates, dest, block_expert, n_used, n_blocks = _route(logits, nt)
        token = jnp.repeat(jnp.arange(nt, dtype=jnp.int32), TOP_K)
        slot_token = jnp.zeros((n_blocks * MOE_BLOCK,), jnp.int32).at[dest].set(token)
        xs = h[slot_token]
        yb_moe = _moe_experts(xs, block_expert, n_used, l, moe_w_gate_up, moe_b_gate_up, moe_w_down, moe_b_down)
        dest_k = dest.reshape(nt, TOP_K)
        y = sum(yb_moe[dest_k[:, k]] * gates[:, k:k + 1] for k in range(TOP_K))
        g2 = jnp.concatenate([jnp.broadcast_to(mod[0:1, 5 * d:], (n_ctx, d)),
                              jnp.broadcast_to(mod[1:2, 5 * d:], (seq, d))], axis=0)
        xt = xt + g2 * y

    return _final_norm(xt, final_norm_w, n_ctx)[None]
```

```python
import functools
import math

import numpy as np
import jax
import jax.numpy as jnp
from jax import lax
from jax.experimental import pallas as pl
from jax.experimental.pallas import tpu as pltpu

F32 = jnp.float32
BF16 = jnp.bfloat16
HIGHEST = lax.Precision.HIGHEST

GRID_W = 64
ROPE_BASE = 10000.0
NORM_EPS = 1e-6
LOG2E = math.log2(math.e)
DIFF_HEADS = 4
DIFF_QK_DIM = 32
DIFF_V_DIM = 64
SSD_HEADS = 4
SSD_HEAD_DIM = 64
SSD_GROUPS = 2
SSD_STATE = 128
SSD_CHUNK = 128
S5_GROUP = 16
S5_GROUPS = 16
S5_STATE = 64
S5_CHUNK = 16
GQA_HEADS = 4
GQA_KV_HEADS = 2
GQA_HEAD_DIM = 64
N_EXPERTS = 32
TOP_K = 4
SWIGLU_LIMIT = 7.0
SWIGLU_ALPHA = 1.702

LANES = 128
ROW_TILE = 256
MOE_BLOCK = 512
KV_CHUNK = 1280
FIRST_CHUNK = 256
VMEM_LIMIT = 56 * 1024 * 1024

C_XBC, C_Z, C_QD, C_QDS, C_KD, C_KDS, C_VD = 0, 768, 1024, 1536, 2048, 2304, 2560
C_QG, C_QGS, C_KG, C_KGS, C_VG, C_DT, C_S5 = 2816, 3072, 3328, 3456, 3584, 3712, 3840
U_COLS = 4096


def _cparams(*sem):
    return pltpu.CompilerParams(dimension_semantics=sem, vmem_limit_bytes=VMEM_LIMIT)


def _rope_partner(d):
    j = np.arange(d)
    first = (j % (d // 2)) < (d // 4)
    return np.where(first, j + d // 4, j - d // 4), np.where(first, -1.0, 1.0)


def _in_proj_columns():
    src = np.full((U_COLS,), -1, np.int64)
    mul = np.zeros((U_COLS,), np.float32)

    def put(dst, cols, sign=None):
        cols = np.asarray(cols)
        src[dst:dst + cols.size] = cols
        mul[dst:dst + cols.size] = 1.0 if sign is None else sign

    pd, sd = _rope_partner(DIFF_QK_DIM)
    pg, sg = _rope_partner(GQA_HEAD_DIM)
    dd = np.arange(DIFF_QK_DIM)
    for h in range(DIFF_HEADS):
        for m in range(2):
            base = h * 64 + m * 32
            dst = h * 128 + (0 if m == 0 else 96)
            put(C_QD + dst, base + dd)
            put(C_QDS + dst, base + pd, sd)
            put(C_KD + base, 256 + base + dd)
            put(C_KDS + base, 256 + base + pd, sd)
    put(C_VD, 512 + np.arange(256))
    ssd0 = 768
    put(C_Z, ssd0 + np.arange(256))
    put(C_XBC, ssd0 + 256 + np.arange(768))
    put(C_DT, ssd0 + 1024 + np.arange(8))
    put(C_S5, 1800 + np.arange(256))
    g0 = 2056
    dg = np.arange(GQA_HEAD_DIM)
    for hq in range(GQA_HEADS):
        put(C_QG + hq * 64, g0 + hq * 64 + dg)
        put(C_QGS + hq * 64, g0 + hq * 64 + pg, sg)
    for hk in range(GQA_KV_HEADS):
        put(C_KG + hk * 64, g0 + 256 + hk * 64 + dg)
        put(C_KGS + hk * 64, g0 + 256 + hk * 64 + pg, sg)
    put(C_VG, g0 + 384 + np.arange(128))
    return src, mul


_IN_SRC, _IN_MUL = _in_proj_columns()


def _rope_tables(seq, n_ctx, d):
    rows = seq // GRID_W
    row = jnp.repeat(jnp.arange(rows, dtype=jnp.int32), GRID_W).astype(F32)
    col = jnp.tile(jnp.arange(GRID_W, dtype=jnp.int32), rows).astype(F32)
    axis_dim = d // 2
    inv_freq = ROPE_BASE ** (-jnp.arange(0, axis_dim, 2, dtype=F32) / axis_dim)
    j = np.arange(d)
    f = (j % axis_dim) % (d // 4)
    is_col = j >= axis_dim
    pos = jnp.where(jnp.asarray(is_col)[None, :], col[:, None], row[:, None])
    ang = pos * inv_freq[f][None, :]
    cos = jnp.tile(jnp.cos(ang), (1, LANES // d))
    sin = jnp.tile(jnp.sin(ang), (1, LANES // d))
    cos = jnp.concatenate([jnp.ones((n_ctx, LANES), F32), cos], axis=0)
    sin = jnp.concatenate([jnp.zeros((n_ctx, LANES), F32), sin], axis=0)
    return cos, sin


def _silu(x):
    return x * jax.nn.sigmoid(x)


def _ada_kernel(cc_ref, w_ref, b_ref, o_ref):
    s = _silu(cc_ref[...])
    o_ref[0] = jnp.dot(s, w_ref[0], precision=HIGHEST, preferred_element_type=F32) + b_ref[0]


def _ada_modulation(cc, w_ada, b_ada):
    depth, d, d6 = w_ada.shape
    return pl.pallas_call(
        _ada_kernel,
        out_shape=jax.ShapeDtypeStruct((depth, 8, d6), F32),
        grid=(depth, d6 // d),
        in_specs=[pl.BlockSpec((8, d), lambda l, j: (0, 0)),
                  pl.BlockSpec((1, d, d), lambda l, j: (l, 0, j)),
                  pl.BlockSpec((1, 1, d), lambda l, j: (l, 0, j))],
        out_specs=pl.BlockSpec((1, 8, d), lambda l, j: (l, 0, j)),
        compiler_params=_cparams("parallel", "parallel"),
        name="ada_modulation",
    )(cc, w_ada, b_ada.reshape(depth, 1, d6))


def _mod_rows(mod_ref, is_ctx, k, d):
    return jnp.where(is_ctx, mod_ref[0:1, k * d:(k + 1) * d], mod_ref[1:2, k * d:(k + 1) * d])


def _rms(x):
    return x * lax.rsqrt(jnp.mean(x * x, axis=-1, keepdims=True) + NORM_EPS)


def _in_proj_kernel(x_ref, mod_ref, nw_ref, w_ref, u_ref, *, ctx_tiles):
    d = x_ref.shape[1]
    is_ctx = pl.program_id(0) < ctx_tiles
    y = _rms(x_ref[...]) * nw_ref[...]
    h = y * (1.0 + _mod_rows(mod_ref, is_ctx, 1, d)) + _mod_rows(mod_ref, is_ctx, 0, d)
    u_ref[...] = jnp.dot(h.astype(BF16), w_ref[...], preferred_element_type=F32)


def _in_proj(x, mod, norm_w, w_ext, n_ctx):
    nt, d = x.shape
    return pl.pallas_call(
        functools.partial(_in_proj_kernel, ctx_tiles=n_ctx // ROW_TILE),
        out_shape=jax.ShapeDtypeStruct((nt, U_COLS), F32),
        grid=(nt // ROW_TILE,),
        in_specs=[pl.BlockSpec((ROW_TILE, d), lambda i: (i, 0)),
                  pl.BlockSpec(mod.shape, lambda i: (0, 0)),
                  pl.BlockSpec((1, d), lambda i: (0, 0)),
                  pl.BlockSpec((d, U_COLS), lambda i: (0, 0))],
        out_specs=pl.BlockSpec((ROW_TILE, U_COLS), lambda i: (i, 0)),
        compiler_params=_cparams("parallel"),
        name="in_proj",
    )(x, mod, norm_w.reshape(1, d), w_ext)


def _attn_prep_kernel(qd_ref, qds_ref, kd_ref, kds_ref, vd_ref, qg_ref, qgs_ref, kg_ref, kgs_ref, vg_ref,
                      cd_ref, sd_ref, cg_ref, sg_ref, nq_ref, nk_ref, seg_ref,
                      q_out, kt_out, v_out):
    cd, sd, cg, sg = cd_ref[...], sd_ref[...], cg_ref[...], sg_ref[...]
    diff_scale = DIFF_QK_DIM ** -0.5 * LOG2E
    gqa_scale = GQA_HEAD_DIM ** -0.5 * LOG2E
    ones_col = (lax.broadcasted_iota(jnp.int32, (qd_ref.shape[0], 64), 1) == 0).astype(BF16)

    def with_ones(v):
        return jnp.concatenate([v.astype(BF16), ones_col], axis=1)

    for h in range(DIFF_HEADS):
        sl = slice(h * LANES, (h + 1) * LANES)
        q = qd_ref[:, sl] * cd + qds_ref[:, sl] * sd
        q_out[h] = (q * diff_scale).astype(BF16)
        v_out[h] = with_ones(vd_ref[:, h * DIFF_V_DIM:(h + 1) * DIFF_V_DIM])
    for j in range(2):
        sl = slice(j * LANES, (j + 1) * LANES)
        k = kd_ref[:, sl] * cd + kds_ref[:, sl] * sd
        kt_out[sl, :] = k.T.astype(BF16)

    seg = seg_ref[...]

    def normed_rope(x, xs, w, ws):
        ms = jnp.dot(x * x, seg, precision=HIGHEST, preferred_element_type=F32)
        return (x * w * cg + xs * ws * sg) * lax.rsqrt(ms + NORM_EPS)

    for g in range(GQA_KV_HEADS):
        sl = slice(g * LANES, (g + 1) * LANES)
        q = normed_rope(qg_ref[:, sl], qgs_ref[:, sl], nq_ref[0:1, :], nq_ref[1:2, :])
        q_out[DIFF_HEADS + g] = (q * gqa_scale).astype(BF16)
        v_out[DIFF_HEADS + g] = with_ones(vg_ref[:, g * GQA_HEAD_DIM:(g + 1) * GQA_HEAD_DIM])
    k = normed_rope(kg_ref[...], kgs_ref[...], nk_ref[0:1, :], nk_ref[1:2, :])
    kt_out[2 * LANES:3 * LANES, :] = k.T.astype(BF16)


def _attn_prep(u, tables, q_norm_w, k_norm_w):
    nt = u.shape[0]
    t = ROW_TILE
    n_grp = DIFF_HEADS + GQA_KV_HEADS
    pg, _ = _rope_partner(GQA_HEAD_DIM)
    nq = jnp.stack([jnp.tile(q_norm_w, 2), jnp.tile(q_norm_w[pg], 2)])
    nk = jnp.stack([jnp.tile(k_norm_w, 2), jnp.tile(k_norm_w[pg], 2)])
    lane = np.arange(LANES)
    seg = jnp.asarray((lane[:, None] // GQA_HEAD_DIM == lane[None, :] // GQA_HEAD_DIM) / GQA_HEAD_DIM, F32)

    def ucol(width, c0):
        return pl.BlockSpec((t, width), lambda i, b=c0 // width: (i, b))

    tab = pl.BlockSpec((t, LANES), lambda i: (i, 0))
    small = lambda a: pl.BlockSpec(a.shape, lambda i: (0, 0))
    return pl.pallas_call(
        _attn_prep_kernel,
        out_shape=(jax.ShapeDtypeStruct((n_grp, nt, LANES), BF16),
                   jax.ShapeDtypeStruct((n_grp * 64, nt), BF16),
                   jax.ShapeDtypeStruct((n_grp, nt, LANES), BF16)),
        grid=(nt // t,),
        in_specs=[ucol(512, C_QD), ucol(512, C_QDS), ucol(256, C_KD), ucol(256, C_KDS), ucol(256, C_VD),
                  ucol(256, C_QG), ucol(256, C_QGS), ucol(128, C_KG), ucol(128, C_KGS), ucol(128, C_VG),
                  tab, tab, tab, tab, small(nq), small(nk), small(seg)],
        out_specs=(pl.BlockSpec((n_grp, t, LANES), lambda i: (0, i, 0)),
                   pl.BlockSpec((n_grp * 64, t), lambda i: (0, i)),
                   pl.BlockSpec((n_grp, t, LANES), lambda i: (0, i, 0))),
        compiler_params=_cparams("parallel"),
        name="attn_prep",
    )(u, u, u, u, u, u, u, u, u, u, *tables, nq, nk, seg)


def _attn_kernel(q_ref, kt_ref, v_ref, lq1_ref, lk1_ref, lq2_ref, lk2_ref, sub_ref, o_ref,
                 m_sc, acc_sc, sa_sc, sb_sc, *, n_ctx, kv_chunk, lam_init):
    g = pl.program_id(0)
    i = pl.program_id(1)
    tq = q_ref.shape[1]
    nt = kt_ref.shape[1]
    q = q_ref[0]
    qs = jnp.concatenate([q[:, :64], q[:, 64:]], axis=0)

    m_sc[...] = jnp.full(m_sc.shape, -1e30, F32)
    acc_sc[...] = jnp.zeros(acc_sc.shape, F32)

    def scores(start, size):
        return jnp.dot(qs, kt_ref[:, pl.ds(start, size)], preferred_element_type=F32)

    def accumulate(s, start, size):
        m_prev = m_sc[...]
        m_new = jnp.maximum(m_prev, jnp.max(s, axis=-1, keepdims=True))
        p = jnp.exp2(s - m_new).astype(BF16)
        acc_sc[...] = jnp.exp2(m_prev - m_new) * acc_sc[...] + jnp.dot(
            p, v_ref[0, pl.ds(start, size), :], preferred_element_type=F32)
        m_sc[...] = m_new

    @pl.when(i < n_ctx // tq)
    def _():
        accumulate(scores(0, n_ctx), 0, n_ctx)

    @pl.when(i >= n_ctx // tq)
    def _():
        first = FIRST_CHUNK
        n_big = (nt - first) // kv_chunk
        last = nt - first - n_big * kv_chunk
        if last == 0:
            n_big, last = n_big - 1, kv_chunk
        last_start = first + n_big * kv_chunk
        big = lambda k: pl.multiple_of(first + k * kv_chunk, LANES)
        s_first = scores(0, first)
        if n_big == 0:
            sa_sc[:, :last] = scores(last_start, last)
            accumulate(s_first, 0, first)
            accumulate(sa_sc[:, :last], last_start, last)
        else:
            sa_sc[...] = scores(first, kv_chunk)
            accumulate(s_first, 0, first)
            n_uniform = (n_big - 1) // 2 if n_big % 2 == 1 else n_big // 2 - 1

            def pair(pi, carry):
                c0 = 2 * pi
                sb_sc[...] = scores(big(c0 + 1), kv_chunk)
                accumulate(sa_sc[...], big(c0), kv_chunk)
                sa_sc[...] = scores(big(c0 + 2), kv_chunk)
                accumulate(sb_sc[...], big(c0 + 1), kv_chunk)
                return carry

            lax.fori_loop(0, n_uniform, pair, 0)
            k0 = 2 * n_uniform
            if n_big - k0 == 2:
                sb_sc[...] = scores(first + (k0 + 1) * kv_chunk, kv_chunk)
                accumulate(sa_sc[...], first + k0 * kv_chunk, kv_chunk)
                sa_sc[:, :last] = scores(last_start, last)
                accumulate(sb_sc[...], first + (k0 + 1) * kv_chunk, kv_chunk)
                accumulate(sa_sc[:, :last], last_start, last)
            else:
                sb_sc[:, :last] = scores(last_start, last)
                accumulate(sa_sc[...], first + k0 * kv_chunk, kv_chunk)
                accumulate(sb_sc[:, :last], last_start, last)

    acc = acc_sc[...]
    out = acc[:, :64] / acc[:, 64:65]
    o1, o2 = out[:tq], out[tq:]

    @pl.when(g < DIFF_HEADS)
    def _():
        lam = (jnp.exp(jnp.sum(lq1_ref[...] * lk1_ref[...], axis=-1, keepdims=True))
               - jnp.exp(jnp.sum(lq2_ref[...] * lk2_ref[...], axis=-1, keepdims=True)) + lam_init)
        a = _rms(o1 - lam * o2) * sub_ref[...] * (1.0 - lam_init)
        o_ref[0] = jnp.concatenate([a, jnp.zeros_like(a)], axis=1)

    @pl.when(g >= DIFF_HEADS)
    def _():
        o_ref[0] = jnp.concatenate([o1, o2], axis=1)


def _pick_chunk(nt):
    assert nt % LANES == 0 and nt >= FIRST_CHUNK + LANES
    return KV_CHUNK


def _attention(q_all, kt_all, v_all, lq1, lk1, lq2, lk2, subln_w, n_ctx, lam_init):
    n_grp, nt, _ = q_all.shape
    tq = ROW_TILE
    kv_chunk = _pick_chunk(nt)
    vec = lambda a: a.reshape(1, -1)
    small = pl.BlockSpec((1, DIFF_QK_DIM), lambda g, i: (0, 0))
    return pl.pallas_call(
        functools.partial(_attn_kernel, n_ctx=n_ctx, kv_chunk=kv_chunk, lam_init=lam_init),
        out_shape=jax.ShapeDtypeStruct((n_grp, nt, LANES), F32),
        grid=(n_grp, nt // tq),
        in_specs=[pl.BlockSpec((1, tq, LANES), lambda g, i: (g, i, 0)),
                  pl.BlockSpec((64, nt), lambda g, i: (g, 0)),
                  pl.BlockSpec((1, nt, LANES), lambda g, i: (g, 0, 0)),
                  small, small, small, small,
                  pl.BlockSpec((1, DIFF_V_DIM), lambda g, i: (0, 0))],
        out_specs=pl.BlockSpec((1, tq, LANES), lambda g, i: (g, i, 0)),
        scratch_shapes=[pltpu.VMEM((2 * tq, 1), F32), pltpu.VMEM((2 * tq, LANES), F32),
                        pltpu.VMEM((2 * tq, kv_chunk), F32), pltpu.VMEM((2 * tq, kv_chunk), F32)],
        compiler_params=_cparams("parallel", "parallel"),
        name="flash_attention",
    )(q_all, kt_all, v_all, vec(lq1), vec(lk1), vec(lq2), vec(lk2), vec(subln_w))


def _ssd_kernel(xf_ref, xfp_ref, xfn_ref, dtf_ref, xb_ref, xbp_ref, xbn_ref, dtb_ref,
                cw_ref, cb_ref, dtbias_ref, alog_ref, dskip_ref, yf_ref, yb_ref, stf_sc, stb_sc,
                *, ctx_chunks, n_chunks):
    j = pl.program_id(0)
    L = SSD_CHUNK
    cb = jnp.where(j < ctx_chunks, ctx_chunks - 1 - j, n_chunks + ctx_chunks - 1 - j)

    @pl.when(j == 0)
    def _():
        stf_sc[...] = jnp.zeros(stf_sc.shape, F32)
        stb_sc[...] = jnp.zeros(stb_sc.shape, F32)

    rows = lax.broadcasted_iota(jnp.int32, (L, 1), 0)
    r2 = lax.broadcasted_iota(jnp.int32, (L, L), 0)
    c2 = lax.broadcasted_iota(jnp.int32, (L, L), 1)
    a_row = -jnp.exp(alog_ref[...])

    def conv_silu(cur_ref, prev_ref, next_ref, c):
        has_prev = jnp.logical_and(c != 0, c != ctx_chunks)
        has_next = jnp.logical_and(c != ctx_chunks - 1, c != n_chunks - 1)
        x = cur_ref[...]
        prev_row = jnp.where(has_prev, prev_ref[7:8, :], 0.0)
        next_row = jnp.where(has_next, next_ref[0:1, :], 0.0)
        x_m1 = jnp.where(rows == 0, prev_row, pltpu.roll(x, 1, axis=0))
        x_p1 = jnp.where(rows == L - 1, next_row, pltpu.roll(x, L - 1, axis=0))
        y = x_m1 * cw_ref[0:1, :] + x * cw_ref[1:2, :] + x_p1 * cw_ref[2:3, :] + cb_ref[...]
        return _silu(y)

    def direction(xa, dt_raw, d, st_sc, y_ref):
        reverse = d == 1
        xs, bm, cm = xa[:, :256], xa[:, 256:512], xa[:, 512:768]
        dt = jax.nn.softplus(dt_raw + dtbias_ref[...])
        adt = dt * a_row
        mask = (c2 >= r2) if reverse else (c2 <= r2)
        cs = jnp.dot(mask.astype(F32), adt, precision=HIGHEST, preferred_element_type=F32)
        cs_t = cs.T
        tot = cs[0:1, :] if reverse else cs[L - 1:L, :]
        for g in range(SSD_GROUPS):
            bg = bm[:, g * SSD_STATE:(g + 1) * SSD_STATE]
            cg = cm[:, g * SSD_STATE:(g + 1) * SSD_STATE].astype(BF16)
            bg_t = bg.T.astype(BF16)
            gmat = jnp.dot(cg, bg_t, preferred_element_type=F32)
            for hh in range(SSD_HEADS // SSD_GROUPS):
                h = g * (SSD_HEADS // SSD_GROUPS) + hh
                lane = d * SSD_HEADS + h
                hs = slice(h * SSD_HEAD_DIM, (h + 1) * SSD_HEAD_DIM)
                cs_col = cs[:, lane:lane + 1]
                cs_row = cs_t[lane:lane + 1, :]
                lmat = jnp.exp(jnp.where(mask, cs_col - cs_row, -1e30))
                xh = xs[:, hs]
                xdt = xh * dt[:, lane:lane + 1]
                y = jnp.dot((gmat * lmat).astype(BF16), xdt.astype(BF16), preferred_element_type=F32)
                st = st_sc[h]
                y = y + jnp.dot(cg, st.astype(BF16), preferred_element_type=F32) * jnp.exp(cs_col)
                tot_h = tot[:, lane:lane + 1]
                dec = jnp.exp(tot_h - cs_col)
                st_sc[h] = jnp.exp(tot_h) * st + jnp.dot(bg_t, (xdt * dec).astype(BF16),
                                                         preferred_element_type=F32)
                if not reverse:
                    y = y + dskip_ref[:, hs] * xh
                y_ref[:, hs] = y

    direction(conv_silu(xf_ref, xfp_ref, xfn_ref, j), dtf_ref[...], 0, stf_sc, yf_ref)
    direction(conv_silu(xb_ref, xbp_ref, xbn_ref, cb), dtb_ref[...], 1, stb_sc, yb_ref)


def _ssd(u, conv_w, conv_b, dt_bias, a_log, d_skip, n_ctx):
    nt = u.shape[0]
    L = SSD_CHUNK
    n_chunks = nt // L
    ctx_chunks = n_ctx // L
    n_oct = nt // 8

    def cb_of(j):
        return jnp.where(j < ctx_chunks, ctx_chunks - 1 - j, n_chunks + ctx_chunks - 1 - j)

    fwd = lambda j: j
    pad8 = lambda v: jnp.pad(v.reshape(1, -1), ((0, 0), (0, LANES - v.size)))

    def specs(cf):
        return [pl.BlockSpec((L, 768), lambda j: (cf(j), C_XBC // 768)),
                pl.BlockSpec((8, 768), lambda j: (jnp.maximum(cf(j) * (L // 8) - 1, 0), C_XBC // 768)),
                pl.BlockSpec((8, 768), lambda j: (jnp.minimum((cf(j) + 1) * (L // 8), n_oct - 1), C_XBC // 768)),
                pl.BlockSpec((L, LANES), lambda j: (cf(j), C_DT // LANES))]

    small = lambda a: pl.BlockSpec(a.shape, lambda j: (0, 0))
    cbias = conv_b.reshape(1, -1)
    dtb = pad8(dt_bias)
    alog = pad8(a_log)
    dsk = jnp.repeat(d_skip, SSD_HEAD_DIM).reshape(1, -1)
    return pl.pallas_call(
        functools.partial(_ssd_kernel, ctx_chunks=ctx_chunks, n_chunks=n_chunks),
        out_shape=(jax.ShapeDtypeStruct((nt, 256), F32), jax.ShapeDtypeStruct((nt, 256), F32)),
        grid=(n_chunks,),
        in_specs=specs(fwd) + specs(cb_of) + [small(conv_w), small(cbias), small(dtb), small(alog), small(dsk)],
        out_specs=(pl.BlockSpec((L, 256), lambda j: (j, 0)),
                   pl.BlockSpec((L, 256), lambda j: (cb_of(j), 0))),
        scratch_shapes=[pltpu.VMEM((SSD_HEADS, SSD_STATE, SSD_HEAD_DIM), F32),
                        pltpu.VMEM((SSD_HEADS, SSD_STATE, SSD_HEAD_DIM), F32)],
        compiler_params=_cparams("arbitrary"),
        name="ssd_scan",
    )(u, u, u, u, u, u, u, u, conv_w, cbias, dtb, alog, dsk)


def _s5_matrices(lam_re, lam_im, log_dt, b_re, b_im, c_re, c_im, d_skip):
    T = S5_CHUNK
    lam = lax.complex(lam_re.astype(F32), lam_im.astype(F32))
    step = jnp.exp(log_dt.astype(F32))[..., None]
    a_bar = jnp.exp(lam * step)
    b_bar = ((a_bar - 1.0) / lam)[..., None] * lax.complex(b_re.astype(F32), b_im.astype(F32))[None]
    c_mat = lax.complex(c_re.astype(F32), c_im.astype(F32))
    k = jnp.arange(T + 1, dtype=F32)
    apow = jnp.exp((lam * step)[..., None] * k)
    kern = jnp.real(jnp.einsum('dgop,dgpt,dgpi->dgtoi', c_mat, apow[..., :T], b_bar, precision=HIGHEST))
    s_idx = np.arange(T)[:, None]
    t_idx = np.arange(T)[None, :]
    lag_f = np.clip(t_idx - s_idx, 0, T - 1)
    lag_b = np.clip(s_idx - t_idx, 0, T - 1)
    kf = jnp.where(jnp.asarray(s_idx <= t_idx)[None, :, :, None, None], kern[0][:, lag_f], 0.0)
    kb = jnp.where(jnp.asarray(s_idx >= t_idx)[None, :, :, None, None], kern[1][:, lag_b], 0.0)
    toep = jnp.transpose(kf + kb, (0, 1, 4, 2, 3)).reshape(S5_GROUPS, T * S5_GROUP, T * S5_GROUP)
    pf = apow[0][..., T - 1 - np.arange(T)]
    pb = apow[1][..., np.arange(T)]
    bsf = jnp.einsum('gps,gpi->gsip', pf, b_bar[0]).reshape(S5_GROUPS, T * S5_GROUP, S5_STATE)
    bsb = jnp.einsum('gps,gpi->gsip', pb, b_bar[1]).reshape(S5_GROUPS, T * S5_GROUP, S5_STATE)

    def ri(z):
        return jnp.concatenate([jnp.real(z), jnp.imag(z)], -1), jnp.concatenate([jnp.imag(z), jnp.real(z)], -1)

    bsf_n, bsf_s = ri(bsf)
    bsb_n, bsb_s = ri(bsb)
    w_a = jnp.concatenate([toep, bsf_n, bsf_s, bsb_n, bsb_s], axis=-1)
    cf = jnp.einsum('gop,gpt->gpto', c_mat[0], apow[0][..., 1 + np.arange(T)]).reshape(S5_GROUPS, S5_STATE, -1)
    cb = jnp.einsum('gop,gpt->gpto', c_mat[1], apow[1][..., T - np.arange(T)]).reshape(S5_GROUPS, S5_STATE, -1)
    w_c = jnp.concatenate([jnp.real(cf), -jnp.imag(cf), jnp.real(cb), -jnp.imag(cb)], axis=1)
    a16 = apow[..., T]
    ar, ai = jnp.real(a16), jnp.imag(a16)
    zeros = jnp.zeros_like(ar[0])
    coef = jnp.stack([jnp.concatenate([ar[0], ar[0]], -1), jnp.concatenate([-ai[0], ai[0]], -1),
                      jnp.concatenate([ai[0], -ai[0]], -1),
                      jnp.concatenate([ar[1], ar[1]], -1), jnp.concatenate([-ai[1], ai[1]], -1),
                      jnp.concatenate([ai[1], -ai[1]], -1),
                      jnp.concatenate([zeros, zeros], -1), jnp.concatenate([zeros, zeros], -1)], axis=1)
    d_row = jnp.tile(d_skip.astype(F32).reshape(S5_GROUPS, 1, S5_GROUP), (1, 1, T))
    return w_a.astype(BF16), w_c.astype(BF16), coef, d_row


def _s5_kernel(u_ref, wa_ref, wc_ref, coef_ref, d_ref, y_ref, s_sc, h_sc, *, ctx_chunks):
    n_chunks = u_ref.shape[1]
    u = u_ref[0]
    sa = jnp.dot(u.astype(BF16), wa_ref[0], preferred_element_type=F32)
    y_ref[0] = sa[:, :256] + d_ref[0] * u
    s_sc[...] = sa[:, 256:]
    coef = coef_ref[0]
    a1f, a2f, a2sf = coef[0:1], coef[1:2], coef[2:3]
    a1b, a2b, a2sb = coef[3:4], coef[4:5], coef[5:6]

    n_oct = n_chunks // 8
    ctx_oct = ctx_chunks // 8
    rows = lax.broadcasted_iota(jnp.int32, (8, 1), 0)

    def body(kb, carry):
        hf, hfs, hb, hbs = carry
        cb = jnp.where(kb < ctx_oct, ctx_oct - 1 - kb, n_oct + ctx_oct - 1 - kb)
        f0 = pl.multiple_of(kb * 8, 8)
        b0 = pl.multiple_of(cb * 8, 8)
        sf = s_sc[pl.ds(f0, 8), 0:256]
        sb = s_sc[pl.ds(b0, 8), 256:512]
        hf_blk = jnp.zeros((8, 128), F32)
        hb_blk = jnp.zeros((8, 128), F32)
        for r in range(8):
            rb = 7 - r
            hf_blk = jnp.where(rows == r, hf, hf_blk)
            hb_blk = jnp.where(rows == rb, hb, hb_blk)
            hf, hfs = (a1f * hf + a2f * hfs + sf[r:r + 1, 0:128],
                       a1f * hfs + a2sf * hf + sf[r:r + 1, 128:256])
            hb, hbs = (a1b * hb + a2b * hbs + sb[rb:rb + 1, 0:128],
                       a1b * hbs + a2sb * hb + sb[rb:rb + 1, 128:256])
        h_sc[pl.ds(f0, 8), 0:128] = hf_blk
        h_sc[pl.ds(b0, 8), 128:256] = hb_blk
        return hf, hfs, hb, hbs

    z = jnp.zeros((1, 128), F32)
    lax.fori_loop(0, n_oct, body, (z, z, z, z))
    y_ref[0] += jnp.dot(h_sc[...].astype(BF16), wc_ref[0], preferred_element_type=F32)


def _s5(u_s5, mats, n_ctx):
    nt = u_s5.shape[0]
    T = S5_CHUNK
    nc = nt // T
    w_a, w_c, coef, d_row = mats
    ug = u_s5.reshape(nc, T, S5_GROUPS, S5_GROUP).transpose(2, 0, 1, 3).reshape(S5_GROUPS, nc, T * S5_GROUP)
    per_g = lambda a: pl.BlockSpec((1,) + a.shape[1:], lambda g: (g, 0, 0))
    y = pl.pallas_call(
        functools.partial(_s5_kernel, ctx_chunks=n_ctx // T),
        out_shape=jax.ShapeDtypeStruct(ug.shape, F32),
        grid=(S5_GROUPS,),
        in_specs=[per_g(ug), per_g(w_a), per_g(w_c), per_g(coef), per_g(d_row)],
        out_specs=per_g(ug),
        scratch_shapes=[pltpu.VMEM((nc, 512), F32), pltpu.VMEM((nc, 256), F32)],
        compiler_params=_cparams("parallel"),
        name="s5_scan",
    )(ug, w_a, w_c, coef, d_row)
    return y.reshape(S5_GROUPS, nc, T, S5_GROUP).transpose(1, 2, 0, 3).reshape(nt, S5_GROUPS * S5_GROUP)


def _out_proj_kernel(x_ref, o_ref, yf_ref, yb_ref, z_ref, y5_ref, mod_ref, ssdn_ref, wglu_ref, bglu_ref,
                     wo_ref, n2_ref, wr_ref, br_ref, xn_ref, h_ref, lg_ref, *, ctx_tiles):
    d = x_ref.shape[1]
    is_ctx = pl.program_id(0) < ctx_tiles
    lb = _rms((yf_ref[...] + yb_ref[...]) * _silu(z_ref[...])) * ssdn_ref[...]
    y5 = jax.nn.gelu(y5_ref[...])
    lc = y5 * jax.nn.sigmoid(jnp.dot(y5.astype(BF16), wglu_ref[...], preferred_element_type=F32) + bglu_ref[...])
    mix = jnp.concatenate([o_ref[k] for k in range(o_ref.shape[0])] + [lb, lc], axis=1).astype(BF16)
    proj = jnp.dot(mix, wo_ref[...], preferred_element_type=F32)
    xn = x_ref[...] + _mod_rows(mod_ref, is_ctx, 2, d) * proj
    xn_ref[...] = xn
    h = _rms(xn) * n2_ref[...] * (1.0 + _mod_rows(mod_ref, is_ctx, 4, d)) + _mod_rows(mod_ref, is_ctx, 3, d)
    h_ref[...] = h
    lg_ref[...] = jnp.dot(h, wr_ref[...], precision=HIGHEST, preferred_element_type=F32) + br_ref[...]


def _out_proj(x, o_attn, yf, yb, u, y5, mod, ssd_norm_w, w_glu, b_glu, w_out_ext, norm2_w, w_router, b_router, n_ctx):
    nt, d = x.shape
    t = ROW_TILE
    n_grp = o_attn.shape[0]
    row = lambda w: pl.BlockSpec((t, w), lambda i: (i, 0))
    small = lambda a: pl.BlockSpec(a.shape, lambda i: (0, 0))
    wr = jnp.pad(w_router, ((0, 0), (0, LANES - w_router.shape[1])))
    br = jnp.pad(b_router.reshape(1, -1), ((0, 0), (0, LANES - b_router.shape[0])), constant_values=-1e30)
    args = (mod, ssd_norm_w.reshape(1, -1), w_glu.astype(BF16), b_glu.reshape(1, -1), w_out_ext,
            norm2_w.reshape(1, -1), wr, br)
    return pl.pallas_call(
        functools.partial(_out_proj_kernel, ctx_tiles=n_ctx // t),
        out_shape=(jax.ShapeDtypeStruct((nt, d), F32), jax.ShapeDtypeStruct((nt, d), F32),
                   jax.ShapeDtypeStruct((nt, LANES), F32)),
        grid=(nt // t,),
        in_specs=[row(d), pl.BlockSpec((n_grp, t, LANES), lambda i: (0, i, 0)), row(256), row(256),
                  pl.BlockSpec((t, 256), lambda i: (i, C_Z // 256)), row(256)] + [small(a) for a in args],
        out_specs=(row(d), row(d), row(LANES)),
        compiler_params=_cparams("parallel"),
        name="out_proj",
    )(x, o_attn, yf, yb, u, y5, *args)


def _moe_kernel(be_ref, nb_ref, x_ref, wgu_ref, bgu_ref, wd_ref, bd_ref, y_ref, wgu_sc, wd2_sc):
    b = pl.program_id(0)
    live = b < nb_ref[0]

    @pl.when(jnp.logical_and(live, jnp.logical_or(b == 0, be_ref[b] != be_ref[jnp.maximum(b - 1, 0)])))
    def _():
        wgu_sc[...] = wgu_ref[0, 0].astype(BF16)
        hi = pltpu.bitcast(wd_ref[0, 0].astype(BF16).astype(F32), jnp.uint32) & jnp.uint32(0xFFFF0000)
        wd2_sc[...] = pltpu.bitcast(hi | (hi >> 16), BF16)

    @pl.when(live)
    def _():
        gu = jnp.dot(x_ref[...].astype(BF16), wgu_sc[...], preferred_element_type=F32) + bgu_ref[0, 0]
        up = pltpu.roll(gu, gu.shape[1] - 1, axis=1)
        g = jnp.minimum(gu, SWIGLU_LIMIT)
        u = jnp.clip(up, -SWIGLU_LIMIT, SWIGLU_LIMIT)
        even = lax.broadcasted_iota(jnp.int32, gu.shape, 1) % 2 == 0
        act = jnp.where(even, g * jax.nn.sigmoid(SWIGLU_ALPHA * g) * (u + 1.0), 0.0)
        y_ref[...] = jnp.dot(act.astype(BF16), wd2_sc[...], preferred_element_type=F32) + bd_ref[0, 0]

    @pl.when(jnp.logical_not(live))
    def _():
        y_ref[...] = jnp.zeros(y_ref.shape, F32)


def _moe_experts(xs, block_expert, n_used, layer, wgu, bgu, wd, bd):
    cap, d = xs.shape
    depth, n_exp, _, de2 = wgu.shape
    n_blocks = cap // MOE_BLOCK
    bgu = bgu.reshape(depth, n_exp, 1, de2)
    bd = bd.reshape(depth, n_exp, 1, d)
    wspec = lambda k, n: pl.BlockSpec((1, 1, k, n), lambda b, be, nb: (layer, be[b], 0, 0))
    grid_spec = pltpu.PrefetchScalarGridSpec(
        num_scalar_prefetch=2,
        grid=(n_blocks,),
        in_specs=[pl.BlockSpec((MOE_BLOCK, d), lambda b, be, nb: (b, 0)),
                  wspec(d, de2), wspec(1, de2), wspec(de2 // 2, d), wspec(1, d)],
        out_specs=pl.BlockSpec((MOE_BLOCK, d), lambda b, be, nb: (b, 0)),
        scratch_shapes=[pltpu.VMEM((d, de2), BF16), pltpu.VMEM((de2, d), BF16)],
    )
    return pl.pallas_call(
        _moe_kernel,
        out_shape=jax.ShapeDtypeStruct((cap, d), F32),
        grid_spec=grid_spec,
        compiler_params=_cparams("arbitrary"),
        name="moe_experts",
    )(block_expert, n_used, xs, wgu, bgu, wd, bd)


def _router_kernel(lg_ref, out_ref, cnt_ref, run_sc):
    i = pl.program_id(0)

    @pl.when(i == 0)
    def _():
        run_sc[...] = jnp.zeros(run_sc.shape, F32)

    lg = lg_ref[...]
    t = lg.shape[0]
    lane = lax.broadcasted_iota(jnp.int32, lg.shape, 1)
    vals, idxs, hots = [], [], []
    for _ in range(TOP_K):
        m = jnp.max(lg, axis=-1, keepdims=True)
        idx = jnp.min(jnp.where(lg == m, lane, LANES), axis=-1, keepdims=True)
        hot = lane == idx
        vals.append(m)
        idxs.append(idx)
        hots.append(hot)
        lg = jnp.where(hot, -3e38, lg)
    member = sum(h.astype(F32) for h in hots)
    r2 = lax.broadcasted_iota(jnp.int32, (t, t), 0)
    c2 = lax.broadcasted_iota(jnp.int32, (t, t), 1)
    before = jnp.dot((c2 < r2).astype(BF16), member.astype(BF16), preferred_element_type=F32) + run_sc[...]
    run_sc[...] = run_sc[...] + jnp.sum(member, axis=0, keepdims=True)
    es = [jnp.exp(v - vals[0]) for v in vals]
    den = sum(es)
    out = jnp.zeros(lg.shape, F32)
    for k in range(TOP_K):
        rank = jnp.sum(jnp.where(hots[k], before, 0.0), axis=-1, keepdims=True)
        out = jnp.where(lane == k, idxs[k].astype(F32), out)
        out = jnp.where(lane == TOP_K + k, es[k] / den, out)
        out = jnp.where(lane == 2 * TOP_K + k, rank, out)
    out_ref[...] = out
    cnt_ref[...] = jnp.broadcast_to(run_sc[...], cnt_ref.shape)


def _route(logits, n_tokens):
    nt = logits.shape[0]
    t = ROW_TILE
    packed, counts = pl.pallas_call(
        _router_kernel,
        out_shape=(jax.ShapeDtypeStruct((nt, LANES), F32), jax.ShapeDtypeStruct((8, LANES), F32)),
        grid=(nt // t,),
        in_specs=[pl.BlockSpec((t, LANES), lambda i: (i, 0))],
        out_specs=(pl.BlockSpec((t, LANES), lambda i: (i, 0)), pl.BlockSpec((8, LANES), lambda i: (0, 0))),
        scratch_shapes=[pltpu.VMEM((1, LANES), F32)],
        compiler_params=_cparams("arbitrary"),
        name="router",
    )(logits)
    top_idx = packed[:, :TOP_K].astype(jnp.int32)
    gates = packed[:, TOP_K:2 * TOP_K]
    rank = packed[:, 2 * TOP_K:3 * TOP_K].astype(jnp.int32)
    counts = counts[0, :N_EXPERTS].astype(jnp.int32)
    m = nt * TOP_K
    padded = (counts + MOE_BLOCK - 1) // MOE_BLOCK * MOE_BLOCK
    pend = jnp.cumsum(padded)
    pstart = pend - padded
    dest = (pstart[top_idx] + rank).reshape(-1)
    n_blocks = -(-(m + N_EXPERTS * (MOE_BLOCK - 1)) // MOE_BLOCK)
    block_start = jnp.arange(n_blocks, dtype=jnp.int32) * MOE_BLOCK
    block_expert = jnp.minimum(jnp.sum((pend[None, :] <= block_start[:, None]).astype(jnp.int32), axis=1),
                               N_EXPERTS - 1)
    n_used = (pend[-1] // MOE_BLOCK).astype(jnp.int32).reshape(1)
    return gates, dest.astype(jnp.int32), block_expert, n_used, n_blocks


def _final_norm_kernel(x_ref, w_ref, o_ref):
    o_ref[...] = _rms(x_ref[...]) * w_ref[...]


def _final_norm(x, w, n_ctx):
    nt, d = x.shape
    t = ROW_TILE
    off = n_ctx // t
    return pl.pallas_call(
        _final_norm_kernel,
        out_shape=jax.ShapeDtypeStruct((nt - n_ctx, d), F32),
        grid=((nt - n_ctx) // t,),
        in_specs=[pl.BlockSpec((t, d), lambda i: (i + off, 0)), pl.BlockSpec((1, d), lambda i: (0, 0))],
        out_specs=pl.BlockSpec((t, d), lambda i: (i, 0)),
        compiler_params=_cparams("parallel"),
        name="final_norm",
    )(x, w.reshape(1, d))


def kernel(x, c, ctx, c_ctx, w_ada, b_ada, norm1_w, norm2_w, w_in, w_out, diff_lq1, diff_lk1, diff_lq2, diff_lk2, diff_subln_w, ssd_conv_w, ssd_conv_b, ssd_dt_bias, ssd_a_log, ssd_d, ssd_norm_w, s5_lam_re, s5_lam_im, s5_log_dt, s5_b_re, s5_b_im, s5_c_re, s5_c_im, s5_d, s5_w_glu, s5_b_glu, gqa_q_norm_w, gqa_k_norm_w, moe_w_router, moe_b_router, moe_w_gate_up, moe_b_gate_up, moe_w_down, moe_b_down, final_norm_w):
    batch, seq, d = x.shape
    assert batch == 1, "kernels are written for a single sequence"
    n_ctx = ctx.shape[1]
    depth = w_in.shape[0]
    nt = n_ctx + seq
    assert n_ctx % ROW_TILE == 0 and seq % ROW_TILE == 0 and seq % GRID_W == 0

    xt = jnp.concatenate([ctx[0], x[0]], axis=0)
    cc = jnp.zeros((8, d), F32).at[0].set(c_ctx).at[1].set(c[0])
    mod_all = _ada_modulation(cc, w_ada, b_ada)
    tables = _rope_tables(seq, n_ctx, DIFF_QK_DIM) + _rope_tables(seq, n_ctx, GQA_HEAD_DIM)

    src = jnp.asarray(np.maximum(_IN_SRC, 0))
    mul = jnp.asarray(_IN_MUL)
    wo_rows = np.full((6 * LANES + 512,), -1, np.int64)
    for h in range(DIFF_HEADS):
        wo_rows[h * LANES:h * LANES + 64] = h * 64 + np.arange(64)
    wo_rows[4 * LANES:6 * LANES] = 768 + np.arange(256)
    wo_rows[6 * LANES:6 * LANES + 512] = 256 + np.arange(512)
    wo_src = jnp.asarray(np.maximum(wo_rows, 0))
    wo_mul = jnp.asarray((wo_rows >= 0).astype(np.float32))

    for l in range(depth):
        lam_init = 0.8 - 0.6 * math.exp(-0.3 * l)
        mod = mod_all[l]
        w_ext = (w_in[l][:, src] * mul[None, :]).astype(BF16)
        w_out_ext = (w_out[l][wo_src, :] * wo_mul[:, None]).astype(BF16)

        u = _in_proj(xt, mod, norm1_w[l], w_ext, n_ctx)
        q_all, kt_all, v_all = _attn_prep(u, tables, gqa_q_norm_w[l], gqa_k_norm_w[l])
        o_attn = _attention(q_all, kt_all, v_all, diff_lq1[l], diff_lk1[l], diff_lq2[l], diff_lk2[l],
                            diff_subln_w[l], n_ctx, lam_init)
        yf, yb = _ssd(u, ssd_conv_w[l], ssd_conv_b[l], ssd_dt_bias[l], ssd_a_log[l], ssd_d[l], n_ctx)
        mats = _s5_matrices(s5_lam_re[l], s5_lam_im[l], s5_log_dt[l], s5_b_re[l], s5_b_im[l],
                            s5_c_re[l], s5_c_im[l], s5_d[l])
        y5 = _s5(u[:, C_S5:C_S5 + 256], mats, n_ctx)
        xt, h, logits = _out_proj(xt, o_attn, yf, yb, u, y5, mod, ssd_norm_w[l], s5_w_glu[l], s5_b_glu[l],
                                  w_out_ext, norm2_w[l], moe_w_router[l], moe_b_router[l], n_ctx)

        gates, dest, block_expert, n_used, n_blocks = _route(logits, nt)
        token = jnp.repeat(jnp.arange(nt, dtype=jnp.int32), TOP_K)
        slot_token = jnp.zeros((n_blocks * MOE_BLOCK,), jnp.int32).at[dest].set(token)
        xs = h[slot_token]
        yb_moe = _moe_experts(xs, block_expert, n_used, l, moe_w_gate_up, moe_b_gate_up, moe_w_down, moe_b_down)
        dest_k = dest.reshape(nt, TOP_K)
        y = sum(yb_moe[dest_k[:, k]] * gates[:, k:k + 1] for k in range(TOP_K))
        g2 = jnp.concatenate([jnp.broadcast_to(mod[0:1, 5 * d:], (n_ctx, d)),
                              jnp.broadcast_to(mod[1:2, 5 * d:], (seq, d))], axis=0)
        xt = xt + g2 * y

    return _final_norm(xt, final_norm_w, n_ctx)[None]
```
